```python
import jax, jax.numpy as jnp
from jax import lax
import numpy as np

D_MODEL = 1024
BATCH = 2
SEQ = 8192
DEPTH = 1

CHUNK = 64
Q_BLOCK = 128
RWKV_WIDTH = D_MODEL // 2
RWKV_HEAD = 64
RWKV_HEADS = RWKV_WIDTH // RWKV_HEAD
DECAY_LORA = 64
AAA_LORA = 64
GATE_LORA = 128
DIFF_WIDTH = D_MODEL - RWKV_WIDTH
DIFF_HEAD = 64
DIFF_HEADS = DIFF_WIDTH // (2 * DIFF_HEAD)
MEM_LEN = 256
XATTN_HEADS = 4
XATTN_HEAD = D_MODEL // XATTN_HEADS
N_EXPERTS = 32
TOP_K = 4
D_FF = D_MODEL
SWIGLU_LIMIT = 7.0
SWIGLU_ALPHA = 1.702
MOE_BLOCK = 256
NORM_EPS = 1e-5
GN_EPS = 64e-5
RWKV_COL_SIZES = (RWKV_WIDTH, RWKV_WIDTH, RWKV_WIDTH, DECAY_LORA, AAA_LORA, GATE_LORA)
SHIFT_COLS = sum(RWKV_COL_SIZES)
IN_COLS = SHIFT_COLS + 3 * DIFF_WIDTH

kernel_name = "hybrid_rwkv7_diffattn_moe_block"


def rms_norm(x, g):
    xf = x.astype(jnp.float32)
    y = xf * lax.rsqrt(jnp.mean(xf * xf, axis=-1, keepdims=True) + NORM_EPS)
    return (y * g.astype(jnp.float32)).astype(x.dtype)


def token_shift(p):
    return jnp.pad(p, ((0, 0), (1, 0), (0, 0)))[:, :-1]


def split_cols(p, sizes):
    idx = np.cumsum(sizes)[:-1].tolist()
    return jnp.split(p, idx, axis=-1)


def rwkv7_group(p, mu, w0, w2, a0, a2, g2, k_k, k_a, r_k, lnx_g, lnx_b):
    B, T, _ = p.shape
    H, N = RWKV_HEADS, RWKV_HEAD
    p = p + (token_shift(p) - p) * mu
    r, k, v, wd, ad, gd = split_cols(p, RWKV_COL_SIZES)
    w = -jax.nn.softplus(-(w0 + jnp.tanh(wd) @ w2)) - 0.5
    decay = jnp.exp(-jnp.exp(w.astype(jnp.float32)))
    a = jax.nn.sigmoid(a0 + ad @ a2)
    g = jax.nn.sigmoid(gd) @ g2
    kk = (k * k_k).reshape(B, T, H, N).astype(jnp.float32)
    kk = kk / jnp.maximum(jnp.sqrt(jnp.sum(kk * kk, axis=-1, keepdims=True)), 1e-12)
    k = k * (1.0 + (a - 1.0) * k_a)
    hs = lambda t: jnp.moveaxis(t.reshape(B, T, H, N).astype(jnp.float32), 1, 0)
    rh, kh, vh, ah = hs(r), hs(k), hs(v), hs(a)
    xs = (rh, hs(decay), kh, vh, jnp.moveaxis(kk, 1, 0), ah)

    def step(S, inp):
        rt, wt, kt, vt, kkt, at = inp
        sa = jnp.einsum('bhvk,bhk->bhv', S, -kkt)
        S = S * wt[:, :, None, :] + sa[..., None] * (kkt * at)[:, :, None, :] + vt[..., None] * kt[:, :, None, :]
        return S, jnp.einsum('bhvk,bhk->bhv', S, rt)

    S0 = jnp.zeros((B, H, N, N), jnp.float32)
    _, y = lax.scan(step, S0, xs)
    mean = jnp.mean(y, axis=-1, keepdims=True)
    var = jnp.mean(jnp.square(y - mean), axis=-1, keepdims=True)
    yn = ((y - mean) * lax.rsqrt(var + GN_EPS)).reshape(T, B, H * N)
    yn = yn * lnx_g.astype(jnp.float32) + lnx_b.astype(jnp.float32)
    yn = jnp.moveaxis(yn, 0, 1).reshape(B, T, H, N)
    bonus = jnp.sum(jnp.moveaxis(rh * kh, 0, 1) * r_k.astype(jnp.float32), axis=-1, keepdims=True) * jnp.moveaxis(vh, 0, 1)
    out = (yn + bonus).reshape(B, T, H * N).astype(p.dtype) * g
    return out


def diff_attn_group(q, k, v, lam_q1, lam_k1, lam_q2, lam_k2, subln_g, lambda_init):
    B, T, _ = q.shape
    H, d = DIFF_HEADS, DIFF_HEAD
    NB = T // Q_BLOCK
    qh = jnp.transpose(q.reshape(B, T, H, 2, d), (0, 2, 3, 1, 4))
    kh = jnp.transpose(k.reshape(B, T, H, 2, d), (0, 2, 3, 1, 4))
    vh = jnp.transpose(v.reshape(B, T, H, 2 * d), (0, 2, 1, 3))
    lam = (jnp.exp(jnp.sum(lam_q1.astype(jnp.float32) * lam_k1.astype(jnp.float32)))
           - jnp.exp(jnp.sum(lam_q2.astype(jnp.float32) * lam_k2.astype(jnp.float32))) + lambda_init)
    scale = d ** -0.5
    qb = jnp.moveaxis(qh.reshape(B, H, 2, NB, Q_BLOCK, d), 3, 0)
    k_chunk = jnp.arange(T) // CHUNK

    def block(args):
        qi, bi = args
        s = jnp.einsum('bhcqd,bhckd->bhcqk', qi, kh).astype(jnp.float32) * scale
        q_chunk = (bi * Q_BLOCK + jnp.arange(Q_BLOCK)) // CHUNK
        allowed = k_chunk[None, :] <= q_chunk[:, None]
        s = jnp.where(allowed, s, -1e30)
        pr = jax.nn.softmax(s, axis=-1)
        attn = pr[:, :, 0] - lam * pr[:, :, 1]
        return jnp.einsum('bhqk,bhkd->bhqd', attn.astype(vh.dtype), vh)

    o = lax.map(block, (qb, jnp.arange(NB)))
    o = jnp.moveaxis(o, 0, 2).reshape(B, H, T, 2 * d)
    o = rms_norm(o, subln_g) * (1.0 - lambda_init)
    return jnp.transpose(o, (0, 2, 1, 3)).reshape(B, T, H * 2 * d)


def cross_attn(hn, memn, wq, wkv, wo):
    B, T, _ = hn.shape
    q = (hn @ wq).reshape(B, T, XATTN_HEADS, XATTN_HEAD)
    kv = memn @ wkv
    km, vm = jnp.split(kv, 2, axis=-1)
    km = km.reshape(B, MEM_LEN, XATTN_HEADS, XATTN_HEAD)
    vm = vm.reshape(B, MEM_LEN, XATTN_HEADS, XATTN_HEAD)
    s = jnp.einsum('bqhd,bkhd->bhqk', q, km).astype(jnp.float32) * (XATTN_HEAD ** -0.5)
    pr = jax.nn.softmax(s, axis=-1).astype(vm.dtype)
    o = jnp.einsum('bhqk,bkhd->bqhd', pr, vm).reshape(B, T, D_MODEL)
    return o @ wo


def moe(xn, router_w, router_b, w1, b1, w2, b2):
    B, T, D = xn.shape
    N = B * T
    xf = xn.reshape(N, D)
    logits = (xf @ router_w + router_b).astype(jnp.float32)
    top_v, top_i = lax.top_k(logits, TOP_K)
    gates = jax.nn.softmax(top_v, axis=-1).astype(xn.dtype)
    M = N * TOP_K
    flat_e = top_i.reshape(M).astype(jnp.int32)
    flat_tok = jnp.repeat(jnp.arange(N, dtype=jnp.int32), TOP_K)
    flat_g = gates.reshape(M)
    order = jnp.argsort(flat_e, stable=True)
    sorted_e = flat_e[order]
    counts = jnp.bincount(flat_e, length=N_EXPERTS).astype(jnp.int32)
    offsets = jnp.cumsum(counts) - counts
    padded_counts = ((counts + MOE_BLOCK - 1) // MOE_BLOCK) * MOE_BLOCK
    padded_end = jnp.cumsum(padded_counts)
    padded_off = padded_end - padded_counts
    dest = padded_off[sorted_e] + (jnp.arange(M, dtype=jnp.int32) - offsets[sorted_e])
    P = M + N_EXPERTS * MOE_BLOCK
    NBLK = P // MOE_BLOCK
    row_tok = jnp.full((P,), N, jnp.int32).at[dest].set(flat_tok[order])
    row_g = jnp.zeros((P,), xn.dtype).at[dest].set(flat_g[order])
    block_start = jnp.arange(NBLK, dtype=jnp.int32) * MOE_BLOCK
    block_e = jnp.minimum(jnp.sum(block_start[:, None] >= padded_end[None, :], axis=1), N_EXPERTS - 1)
    x_pad = jnp.concatenate([xf, jnp.zeros((1, D), xf.dtype)], axis=0)
    xin = x_pad[row_tok].reshape(NBLK, MOE_BLOCK, D)

    def expert_block(args):
        xb, e = args
        h = xb @ w1[e] + b1[e]
        gate, lin = jnp.split(h, 2, axis=-1)
        gate = jnp.minimum(gate, SWIGLU_LIMIT)
        lin = jnp.clip(lin, -SWIGLU_LIMIT, SWIGLU_LIMIT)
        act = gate * jax.nn.sigmoid(SWIGLU_ALPHA * gate) * (lin + 1.0)
        return act @ w2[e] + b2[e]

    yb = lax.map(expert_block, (xin, block_e)).reshape(P, D) * row_g[:, None]
    y = jax.ops.segment_sum(yb, row_tok, num_segments=N + 1)[:N]
    return y.reshape(B, T, D)


def setup_inputs(seed: int = 0) -> dict:
    key = jax.random.key(seed)
    ks = iter(jax.random.split(key, 40))
    L, D, E, F = DEPTH, D_MODEL, N_EXPERTS, D_FF
    nrm = lambda shape, s: jax.random.normal(next(ks), shape, jnp.float32) * s
    ratio = jnp.linspace(0.0, 1.0, RWKV_WIDTH, dtype=jnp.float32)
    return {
        "x": nrm((BATCH, SEQ, D), 1.0),
        "mem": nrm((BATCH, MEM_LEN, D), 1.0),
        "norm1_g": 1.0 + nrm((L, D), 0.02),
        "w_in": nrm((L, D, IN_COLS), D ** -0.5),
        "mu_shift": jax.random.uniform(next(ks), (L, SHIFT_COLS), jnp.float32),
        "w0": (-6.5 + 5.0 * ratio ** 0.9)[None, :] + nrm((L, RWKV_WIDTH), 0.1),
        "w2": nrm((L, DECAY_LORA, RWKV_WIDTH), 0.1 * DECAY_LORA ** -0.5),
        "a0": nrm((L, RWKV_WIDTH), 0.1),
        "a2": nrm((L, AAA_LORA, RWKV_WIDTH), AAA_LORA ** -0.5),
        "g2": nrm((L, GATE_LORA, RWKV_WIDTH), GATE_LORA ** -0.5),
        "k_k": 0.85 + nrm((L, RWKV_WIDTH), 0.02),
        "k_a": 1.0 + nrm((L, RWKV_WIDTH), 0.02),
        "r_k": nrm((L, RWKV_HEADS, RWKV_HEAD), 0.1),
        "lnx_g": 1.0 + nrm((L, RWKV_WIDTH), 0.02),
        "lnx_b": nrm((L, RWKV_WIDTH), 0.02),
        "lam_q1": nrm((L, DIFF_HEAD), 0.1),
        "lam_k1": nrm((L, DIFF_HEAD), 0.1),
        "lam_q2": nrm((L, DIFF_HEAD), 0.1),
        "lam_k2": nrm((L, DIFF_HEAD), 0.1),
        "subln_g": 1.0 + nrm((L, 2 * DIFF_HEAD), 0.02),
        "w_out": nrm((L, D, D), D ** -0.5),
        "norm2_g": 1.0 + nrm((L, D), 0.02),
        "mem_norm_g": 1.0 + nrm((L, D), 0.02),
        "wq_x": nrm((L, D, D), D ** -0.5),
        "wkv_x": nrm((L, D, 2 * D), D ** -0.5),
        "wo_x": nrm((L, D, D), D ** -0.5),
        "norm3_g": 1.0 + nrm((L, D), 0.02),
        "router_w": nrm((L, D, E), D ** -0.5),
        "router_b": nrm((L, E), 0.01),
        "moe_w1": nrm((L, E, D, 2 * F), D ** -0.5),
        "moe_b1": nrm((L, E, 2 * F), 0.01),
        "moe_w2": nrm((L, E, F, D), F ** -0.5),
        "moe_b2": nrm((L, E, D), 0.01),
        "final_g": 1.0 + nrm((D,), 0.02),
    }


def reference(x, mem, norm1_g, w_in, mu_shift, w0, w2, a0, a2, g2, k_k, k_a, r_k, lnx_g, lnx_b,
              lam_q1, lam_k1, lam_q2, lam_k2, subln_g, w_out, norm2_g, mem_norm_g, wq_x, wkv_x, wo_x,
              norm3_g, router_w, router_b, moe_w1, moe_b1, moe_w2, moe_b2, final_g):
    h = x
    for i in range(DEPTH):
        lambda_init = 0.8 - 0.6 * float(np.exp(-0.3 * i))
        xn = rms_norm(h, norm1_g[i])
        proj = xn @ w_in[i]
        p_rwkv = proj[..., :SHIFT_COLS]
        q_d, k_d, v_d = split_cols(proj[..., SHIFT_COLS:], (DIFF_WIDTH, DIFF_WIDTH, DIFF_WIDTH))
        y_rwkv = rwkv7_group(p_rwkv, mu_shift[i], w0[i], w2[i], a0[i], a2[i], g2[i],
                             k_k[i], k_a[i], r_k[i], lnx_g[i], lnx_b[i])
        y_diff = diff_attn_group(q_d, k_d, v_d, lam_q1[i], lam_k1[i], lam_q2[i], lam_k2[i],
                                 subln_g[i], lambda_init)
        h = h + jnp.concatenate([y_rwkv, y_diff], axis=-1) @ w_out[i]
        h = h + cross_attn(rms_norm(h, norm2_g[i]), rms_norm(mem, mem_norm_g[i]), wq_x[i], wkv_x[i], wo_x[i])
        h = h + moe(rms_norm(h, norm3_g[i]), router_w[i], router_b[i], moe_w1[i], moe_b1[i], moe_w2[i], moe_b2[i])
    return rms_norm(h, final_g)
```

```python
import functools
import math

import jax
import jax.numpy as jnp
from jax import lax
from jax.experimental import pallas as pl
from jax.experimental.pallas import tpu as pltpu

F32 = jnp.float32
BF16 = jnp.bfloat16

CHUNK = 64
RWKV_HEAD = 64
DECAY_LORA = 64
AAA_LORA = 64
GATE_LORA = 128
DIFF_HEAD = 64
XATTN_HEADS = 4
N_EXPERTS = 32
TOP_K = 4
SWIGLU_LIMIT = 7.0
SWIGLU_ALPHA = 1.702
MOE_BLOCK = 256
NORM_EPS = 1e-5
GN_EPS = 64e-5
LAMBDA_INIT = 0.8 - 0.6 * math.exp(-0.3 * 0)

VMEM_LIMIT = 56 * 1024 * 1024
TM_PROJ = 512
TT_RWKV = 256
TQ_ATT = 512
TK_ATT = 512
TM_MID = 512
TM_DISP = 512
TM_COMB = 256


def _cparams(sem):
    return pltpu.CompilerParams(dimension_semantics=sem, vmem_limit_bytes=VMEM_LIMIT)


def _dot(a, b):
    return jnp.dot(a, b, preferred_element_type=F32)


def _dot_nt(a, b):
    return lax.dot_general(a, b, (((1,), (1,)), ((), ())), preferred_element_type=F32)


def _dot_tn(a, b):
    return lax.dot_general(a, b, (((0,), (0,)), ((), ())), preferred_element_type=F32)


def _rms(x, g):
    return x * lax.rsqrt(jnp.mean(x * x, axis=-1, keepdims=True) + NORM_EPS) * g


def _split2(x):
    hi = x.astype(BF16)
    lo = (x - hi.astype(F32)).astype(BF16)
    return hi, lo


def _split3(x):
    hi = x.astype(BF16)
    r1 = x - hi.astype(F32)
    mid = r1.astype(BF16)
    lo = (r1 - mid.astype(F32)).astype(BF16)
    return hi, mid, lo


def _inproj_kernel(x_ref, g_ref, wa_ref, wb_ref, p_ref, qkv_ref):
    xn = _rms(x_ref[...], g_ref[...]).astype(BF16)
    p_ref[...] = _dot(xn, wa_ref[...])
    qkv_ref[...] = _dot(xn, wb_ref[...]).astype(BF16)


def _inproj(x2, g, w_in, shift_cols):
    n, d = x2.shape
    cols = w_in.shape[1]
    wa = w_in[:, :shift_cols].astype(BF16)
    wb = w_in[:, shift_cols:].astype(BF16)
    tm = TM_PROJ
    return pl.pallas_call(
        _inproj_kernel,
        grid=(n // tm,),
        in_specs=[
            pl.BlockSpec((tm, d), lambda i: (i, 0)),
            pl.BlockSpec((1, d), lambda i: (0, 0)),
            pl.BlockSpec((d, shift_cols), lambda i: (0, 0)),
            pl.BlockSpec((d, cols - shift_cols), lambda i: (0, 0)),
        ],
        out_specs=[
            pl.BlockSpec((tm, shift_cols), lambda i: (i, 0)),
            pl.BlockSpec((tm, cols - shift_cols), lambda i: (i, 0)),
        ],
        out_shape=[
            jax.ShapeDtypeStruct((n, shift_cols), F32),
            jax.ShapeDtypeStruct((n, cols - shift_cols), BF16),
        ],
        compiler_params=_cparams(("arbitrary",)),
        name="inproj",
    )(x2, g.reshape(1, d), wa, wb)


def _rwkv_kernel(p_ref, mu_ref, w0_ref, a0_ref, wwa_ref, g2_ref, kk_ref, ka_ref, rk_ref,
                 lg_ref, lb_ref, seg_ref, tri_ref, blk_ref, o_ref,
                 s_ref, last_ref, at_ref, bt_ref, kt_ref, rt_ref, bh_ref, kh_ref, v_ref,
                 gc_ref, t_ref, arb_ref, av_ref, arkv_ref, vk_ref, y_ref, bonus_ref, gate_ref,
                 *, width, heads):
    t_idx = pl.program_id(1)
    tt = p_ref.shape[0]
    nchunk = tt // CHUNK
    hd = RWKV_HEAD

    @pl.when(t_idx == 0)
    def _():
        s_ref[...] = jnp.zeros(s_ref.shape, F32)
        last_ref[...] = jnp.zeros(last_ref.shape, F32)

    p = p_ref[...]
    prev = pltpu.roll(p, shift=1, axis=0)
    row0 = lax.broadcasted_iota(jnp.int32, (tt, 1), 0) == 0
    prev = jnp.where(row0, last_ref[0:1, :], prev)
    last_ref[0:1, :] = p[tt - 1:tt, :]
    pm = p + (prev - p) * mu_ref[...]

    r = pm[:, 0:width]
    k = pm[:, width:2 * width]
    v = pm[:, 2 * width:3 * width]
    wa = pm[:, 3 * width:3 * width + DECAY_LORA + AAA_LORA]
    gd = pm[:, 3 * width + DECAY_LORA + AAA_LORA:]

    lane = lax.broadcasted_iota(jnp.int32, wa.shape, 1)
    z = jnp.where(lane < DECAY_LORA, jnp.tanh(wa), wa).astype(BF16)
    lora = _dot(z, wwa_ref[...])
    logw = -math.exp(-0.5) * jax.nn.sigmoid(w0_ref[...] + lora[:, :width])
    lr = jax.nn.sigmoid(a0_ref[...] + lora[:, width:])
    gate_ref[...] = _dot(jax.nn.sigmoid(gd).astype(BF16), g2_ref[...])

    seg = seg_ref[...]

    def segsum(x):
        hi, lo = _split2(x)
        return _dot(hi, seg) + _dot(lo, seg)

    kk = k * kk_ref[...]
    kk = kk / jnp.maximum(jnp.sqrt(segsum(kk * kk)), 1e-12)
    k2 = k * (1.0 + (lr - 1.0) * ka_ref[...])
    bonus_ref[...] = segsum(r * k2 * rk_ref[...]) * v

    h3 = _split3(logw)
    tri = tri_ref[...]
    blk = blk_ref[...]
    cum = _dot(tri, h3[0]) + _dot(tri, h3[1]) + _dot(tri, h3[2])
    tot = _dot(blk, h3[0]) + _dot(blk, h3[1]) + _dot(blk, h3[2])
    einv = jnp.exp(-cum)
    ed = jnp.exp(tot - cum)
    b = kk * lr
    full = {
        "at": -kk * jnp.exp(cum - logw),
        "bt": b * einv,
        "kt": k2 * einv,
        "rt": r * jnp.exp(cum),
        "bh": b * ed,
        "kh": k2 * ed,
        "v": v,
    }
    refs = {"at": at_ref, "bt": bt_ref, "kt": kt_ref, "rt": rt_ref, "bh": bh_ref, "kh": kh_ref,
            "v": v_ref}
    gc = jnp.exp(tot)
    for h in range(heads):
        sl = slice(h * hd, (h + 1) * hd)
        for name, val in full.items():
            refs[name][h] = val[:, sl].astype(BF16)
        gc_ref[h] = gc[:, sl]

    ri = lax.broadcasted_iota(jnp.int32, (CHUNK, CHUNK), 0)
    ci = lax.broadcasted_iota(jnp.int32, (CHUNK, CHUNK), 1)
    strict = ci < ri
    incl = ci <= ri
    eye = (ci == ri).astype(F32)

    def pre_body(c, carry):
        r0 = pl.multiple_of(c * CHUNK, CHUNK)
        rows = pl.ds(r0, CHUNK)
        for h in range(heads):
            a_t = at_ref[h, rows, :]
            b_t = bt_ref[h, rows, :]
            k_t = kt_ref[h, rows, :]
            r_t = rt_ref[h, rows, :]
            vb = v_ref[h, rows, :]
            aab = jnp.where(strict, _dot_nt(a_t, b_t), 0.0)
            aak = jnp.where(strict, _dot_nt(a_t, k_t), 0.0)
            arb = jnp.where(incl, _dot_nt(r_t, b_t), 0.0)
            ark = jnp.where(incl, _dot_nt(r_t, k_t), 0.0)
            pw = aab
            tinv = eye + aab
            for _ in range(5):
                pwb = pw.astype(BF16)
                pw = _dot(pwb, pwb)
                tinv = tinv + _dot(tinv.astype(BF16), pw.astype(BF16))
            t_ref[h, rows, :] = tinv.astype(BF16)
            arb_ref[h, rows, :] = arb.astype(BF16)
            av_ref[h, rows, :] = _dot(aak.astype(BF16), vb)
            arkv_ref[h, rows, :] = _dot(ark.astype(BF16), vb)
            vk_ref[h, rows, :] = _dot_tn(vb, kh_ref[h, rows, :])
        return carry

    lax.fori_loop(0, nchunk, pre_body, 0)

    def seq_body(c, carry):
        r0 = pl.multiple_of(c * CHUNK, CHUNK)
        rows = pl.ds(r0, CHUNK)
        for h in range(heads):
            s0 = s_ref[h]
            s0b = s0.astype(BF16)
            rhs = _dot_nt(at_ref[h, rows, :], s0b) + av_ref[h, rows, :]
            u = _dot(t_ref[h, rows, :], rhs.astype(BF16))
            ub = u.astype(BF16)
            y = _dot_nt(rt_ref[h, rows, :], s0b) + _dot(arb_ref[h, rows, :], ub) + arkv_ref[h, rows, :]
            y_ref[h, rows, :] = y
            s_ref[h] = (s0 * gc_ref[h, pl.ds(r0, 1), :] + _dot_tn(ub, bh_ref[h, rows, :])
                        + vk_ref[h, rows, :])
        return carry

    lax.fori_loop(0, nchunk, seq_body, 0)

    y = jnp.concatenate([y_ref[h] for h in range(heads)], axis=-1)
    mean = segsum(y) * (1.0 / hd)
    dlt = y - mean
    var = segsum(dlt * dlt) * (1.0 / hd)
    yn = dlt * lax.rsqrt(var + GN_EPS) * lg_ref[...] + lb_ref[...]
    o_ref[...] = ((yn + bonus_ref[...]) * gate_ref[...]).astype(o_ref.dtype)


def _rwkv(p, batch, seq, mu, w0, w2, a0, a2, g2, k_k, k_a, r_k, lnx_g, lnx_b):
    n, cols = p.shape
    width = w0.shape[0]
    heads = width // RWKV_HEAD
    tt = TT_RWKV
    nt = seq // tt
    wwa = jnp.zeros((DECAY_LORA + AAA_LORA, 2 * width), F32)
    wwa = wwa.at[:DECAY_LORA, :width].set(w2).at[DECAY_LORA:, width:].set(a2).astype(BF16)
    lane_head = jnp.arange(width) // RWKV_HEAD
    seg = (lane_head[:, None] == lane_head[None, :]).astype(BF16)
    tch = jnp.arange(tt) // CHUNK
    same = tch[:, None] == tch[None, :]
    tri = (same & (jnp.arange(tt)[None, :] <= jnp.arange(tt)[:, None])).astype(BF16)
    blk = same.astype(BF16)
    row = lambda a: a.reshape(1, -1)
    vec = lambda w: pl.BlockSpec((1, w), lambda b, t: (0, 0))
    full2 = lambda a: pl.BlockSpec(a.shape, lambda b, t: (0, 0))
    hbuf = lambda dt: pltpu.VMEM((heads, tt, RWKV_HEAD), dt)
    kern = functools.partial(_rwkv_kernel, width=width, heads=heads)
    return pl.pallas_call(
        kern,
        grid=(batch, nt),
        in_specs=[
            pl.BlockSpec((tt, cols), lambda b, t: (b * nt + t, 0)),
            vec(cols), vec(width), vec(width), full2(wwa), pl.BlockSpec(g2.shape, lambda b, t: (0, 0)),
            vec(width), vec(width), vec(width), vec(width), vec(width),
            full2(seg), full2(tri), full2(blk),
        ],
        out_specs=pl.BlockSpec((tt, width), lambda b, t: (b * nt + t, 0)),
        out_shape=jax.ShapeDtypeStruct((n, width), BF16),
        scratch_shapes=[
            pltpu.VMEM((heads, RWKV_HEAD, RWKV_HEAD), F32),
            pltpu.VMEM((8, cols), F32),
            hbuf(BF16), hbuf(BF16), hbuf(BF16), hbuf(BF16), hbuf(BF16), hbuf(BF16), hbuf(BF16),
            hbuf(F32),
            hbuf(BF16), hbuf(BF16), hbuf(F32), hbuf(F32), hbuf(F32),
            hbuf(F32),
            pltpu.VMEM((tt, width), F32), pltpu.VMEM((tt, width), F32),
        ],
        compiler_params=_cparams(("arbitrary", "arbitrary")),
        name="rwkv",
    )(p, row(mu), row(w0), row(a0), wwa, g2.astype(BF16), row(k_k), row(k_a), row(r_k),
      row(lnx_g), row(lnx_b), seg, tri, blk)


def _diff_kernel(lq1_ref, lk1_ref, lq2_ref, lk2_ref, sg_ref, q_ref, k_ref, v_ref, o_ref,
                 m0_ref, l0_ref, a0_ref, m1_ref, l1_ref, a1_ref):
    i = pl.program_id(2)
    tq = q_ref.shape[0]
    tk = TK_ATT
    d = DIFF_HEAD
    q = q_ref[...] * (d ** -0.5)
    lane = lax.broadcasted_iota(jnp.int32, q.shape, 1)
    zero = jnp.zeros_like(q)
    q0 = jnp.where(lane < d, q, zero)
    q1 = jnp.where(lane >= d, q, zero)

    for m_ref, l_ref, a_ref in ((m0_ref, l0_ref, a0_ref), (m1_ref, l1_ref, a1_ref)):
        m_ref[...] = jnp.full(m_ref.shape, -1e30, F32)
        l_ref[...] = jnp.zeros(l_ref.shape, F32)
        a_ref[...] = jnp.zeros(a_ref.shape, F32)

    def update(s, vb, m_ref, l_ref, a_ref):
        m_old = m_ref[...]
        m_new = jnp.maximum(m_old, jnp.max(s, axis=-1, keepdims=True))
        alpha = jnp.exp(m_old - m_new)
        pr = jnp.exp(s - m_new)
        l_ref[...] = alpha * l_ref[...] + jnp.sum(pr, axis=-1, keepdims=True)
        a_ref[...] = alpha * a_ref[...] + _dot(pr.astype(BF16), vb)
        m_ref[...] = m_new

    def block(j, masked):
        rows = pl.ds(pl.multiple_of(j * tk, tk), tk)
        kb = k_ref[rows, :]
        vb = v_ref[rows, :]
        s0 = _dot_nt(q0, kb)
        s1 = _dot_nt(q1, kb)
        if masked:
            qc = lax.broadcasted_iota(jnp.int32, (tq, tk), 0) // CHUNK
            kc = lax.broadcasted_iota(jnp.int32, (tq, tk), 1) // CHUNK
            allowed = kc <= qc
            s0 = jnp.where(allowed, s0, -1e30)
            s1 = jnp.where(allowed, s1, -1e30)
        update(s0, vb, m0_ref, l0_ref, a0_ref)
        update(s1, vb, m1_ref, l1_ref, a1_ref)

    def body(j, carry):
        block(j, False)
        return carry

    lax.fori_loop(0, i, body, 0)
    block(i, True)

    lam = (jnp.exp(jnp.sum(lq1_ref[...] * lk1_ref[...], axis=-1, keepdims=True))
           - jnp.exp(jnp.sum(lq2_ref[...] * lk2_ref[...], axis=-1, keepdims=True)) + LAMBDA_INIT)
    o = a0_ref[...] / l0_ref[...] - lam * (a1_ref[...] / l1_ref[...])
    o = _rms(o, sg_ref[...]) * (1.0 - LAMBDA_INIT)
    o_ref[...] = o.astype(o_ref.dtype)


def _diffattn(qkv, batch, seq, lq1, lk1, lq2, lk2, subln_g):
    n = qkv.shape[0]
    width = qkv.shape[1] // 3
    hw = 2 * DIFF_HEAD
    heads = width // hw
    tq = TQ_ATT
    assert TK_ATT == tq and seq % tq == 0
    nq = seq // tq
    vec = lambda w: pl.BlockSpec((1, w), lambda b, h, i: (0, 0))
    row = lambda a: a.reshape(1, -1)
    acc = lambda w: pltpu.VMEM((tq, w), F32)
    return pl.pallas_call(
        _diff_kernel,
        grid=(batch, heads, nq),
        in_specs=[
            vec(DIFF_HEAD), vec(DIFF_HEAD), vec(DIFF_HEAD), vec(DIFF_HEAD), vec(hw),
            pl.BlockSpec((tq, hw), lambda b, h, i: (b * nq + i, h)),
            pl.BlockSpec((seq, hw), lambda b, h, i: (b, heads + h)),
            pl.BlockSpec((seq, hw), lambda b, h, i: (b, 2 * heads + h)),
        ],
        out_specs=pl.BlockSpec((tq, hw), lambda b, h, i: (b * nq + i, h)),
        out_shape=jax.ShapeDtypeStruct((n, width), BF16),
        scratch_shapes=[acc(1), acc(1), acc(hw), acc(1), acc(1), acc(hw)],
        compiler_params=_cparams(("arbitrary", "arbitrary", "arbitrary")),
        name="diffattn",
    )(row(lq1), row(lk1), row(lq2), row(lk2), row(subln_g), qkv, qkv, qkv)


def _memkv_kernel(m_ref, g_ref, w_ref, o_ref):
    o_ref[...] = _dot(_rms(m_ref[...], g_ref[...]).astype(BF16), w_ref[...]).astype(o_ref.dtype)


def _memkv(mem2, g, wkv):
    n, d = mem2.shape
    c = wkv.shape[1]
    tn = 512
    return pl.pallas_call(
        _memkv_kernel,
        grid=(c // tn,),
        in_specs=[
            pl.BlockSpec((n, d), lambda j: (0, 0)),
            pl.BlockSpec((1, d), lambda j: (0, 0)),
            pl.BlockSpec((d, tn), lambda j: (0, j)),
        ],
        out_specs=pl.BlockSpec((n, tn), lambda j: (0, j)),
        out_shape=jax.ShapeDtypeStruct((n, c), BF16),
        compiler_params=_cparams(("arbitrary",)),
        name="memkv",
    )(mem2, g.reshape(1, d), wkv.astype(BF16))


def _mid_kernel(x_ref, yr_ref, yd_ref, wo1_ref, wo2_ref, g2_ref, wq_ref, km_ref, vm_ref, wox_ref,
                g3_ref, rw_ref, rb_ref, tril_ref,
                h_ref, xn_ref, ti_ref, gt_ref, rk_ref, cnt_ref, base_ref):
    step = pl.program_id(0) * pl.num_programs(1) + pl.program_id(1)
    tm, dm = x_ref.shape
    hd = dm // XATTN_HEADS

    @pl.when(step == 0)
    def _():
        base_ref[...] = jnp.zeros(base_ref.shape, F32)

    h1 = x_ref[...] + _dot(yr_ref[...], wo1_ref[...]) + _dot(yd_ref[...], wo2_ref[...])
    hn = _rms(h1, g2_ref[...]).astype(BF16)
    q = (_dot(hn, wq_ref[...]) * (hd ** -0.5)).astype(BF16)
    outs = []
    for hh in range(XATTN_HEADS):
        sl = slice(hh * hd, (hh + 1) * hd)
        s = _dot_nt(q[:, sl], km_ref[:, sl])
        s = s - jnp.max(s, axis=-1, keepdims=True)
        e = jnp.exp(s)
        pr = e / jnp.sum(e, axis=-1, keepdims=True)
        outs.append(_dot(pr.astype(BF16), vm_ref[:, sl]))
    o = jnp.concatenate(outs, axis=-1).astype(BF16)
    h2 = h1 + _dot(o, wox_ref[...])
    h_ref[...] = h2
    xn = _rms(h2, g3_ref[...])
    xn_ref[...] = xn

    xs = _split3(xn)
    w_hi, w_mid, w_lo = rw_ref[0], rw_ref[1], rw_ref[2]
    logits = (_dot(xs[0], w_hi) + (_dot(xs[0], w_mid) + _dot(xs[1], w_hi))
              + (_dot(xs[0], w_lo) + _dot(xs[1], w_mid) + _dot(xs[2], w_hi))) + rb_ref[...]

    ne = logits.shape[1]
    eio = lax.broadcasted_iota(jnp.int32, (tm, ne), 1).astype(F32)
    work = logits
    vals, idxs, sels = [], [], []
    for _ in range(TOP_K):
        mx = jnp.max(work, axis=-1, keepdims=True)
        ix = jnp.min(jnp.where(work == mx, eio, float(ne)), axis=-1, keepdims=True)
        sel = eio == ix
        vals.append(mx)
        idxs.append(ix)
        sels.append(sel)
        work = jnp.where(sel, -jnp.inf, work)
    es = [jnp.exp(vv - vals[0]) for vv in vals]
    den = es[0] + es[1] + es[2] + es[3]
    gates = [ee / den for ee in es]

    assign = jnp.zeros((tm, ne), F32)
    for sel in sels:
        assign = assign + sel.astype(F32)
    before = _dot(tril_ref[...], assign.astype(BF16)) + base_ref[...]
    ranks = [jnp.sum(jnp.where(sel, before, 0.0), axis=-1, keepdims=True) for sel in sels]
    base_new = base_ref[...] + jnp.sum(assign, axis=0, keepdims=True)
    base_ref[...] = base_new
    cnt_ref[...] = base_new.astype(jnp.int32)

    kio = lax.broadcasted_iota(jnp.int32, (tm, TOP_K), 1)

    def cols(parts):
        out = jnp.where(kio == 0, parts[0], parts[1])
        for kk in range(2, TOP_K):
            out = jnp.where(kio == kk, parts[kk], out)
        return out

    ti_ref[...] = cols(idxs).astype(jnp.int32)
    gt_ref[...] = cols(gates)
    rk_ref[...] = cols(ranks).astype(jnp.int32)


def _mid(x2, yr, yd, batch, seq, w_out, norm2_g, wq, kv, wo, norm3_g, router_w, router_b):
    n, d = x2.shape
    tm = TM_MID
    nt = seq // tm
    mem_len = kv.shape[0] // batch
    wr = yr.shape[1]
    wo1 = w_out[:wr].astype(BF16)
    wo2 = w_out[wr:].astype(BF16)
    rw = jnp.stack(_split3(router_w))
    ne = router_w.shape[1]
    tril = (jnp.arange(tm)[None, :] < jnp.arange(tm)[:, None]).astype(BF16)
    tok = lambda w: pl.BlockSpec((tm, w), lambda b, t: (b * nt + t, 0))
    cst = lambda a: pl.BlockSpec(a.shape, lambda b, t: (0,) * a.ndim)
    row = lambda a: a.reshape(1, -1)
    return pl.pallas_call(
        _mid_kernel,
        grid=(batch, nt),
        in_specs=[
            tok(d), tok(wr), tok(d - wr), cst(wo1), cst(wo2), cst(row(norm2_g)), cst(wq),
            pl.BlockSpec((mem_len, d), lambda b, t: (b, 0)),
            pl.BlockSpec((mem_len, d), lambda b, t: (b, 1)),
            cst(wo), cst(row(norm3_g)), cst(rw), cst(row(router_b)), cst(tril),
        ],
        out_specs=[tok(d), tok(d), tok(TOP_K), tok(TOP_K), tok(TOP_K),
                   pl.BlockSpec((1, ne), lambda b, t: (0, 0))],
        out_shape=[
            jax.ShapeDtypeStruct((n, d), F32),
            jax.ShapeDtypeStruct((n, d), F32),
            jax.ShapeDtypeStruct((n, TOP_K), jnp.int32),
            jax.ShapeDtypeStruct((n, TOP_K), F32),
            jax.ShapeDtypeStruct((n, TOP_K), jnp.int32),
            jax.ShapeDtypeStruct((1, ne), jnp.int32),
        ],
        scratch_shapes=[pltpu.VMEM((1, ne), F32)],
        compiler_params=_cparams(("arbitrary", "arbitrary")),
        name="mid",
    )(x2, yr, yd, wo1, wo2, row(norm2_g), wq.astype(BF16), kv, kv, wo.astype(BF16),
      row(norm3_g), rw, row(router_b), tril)


def _dispatch_kernel(dest_ref, x_ref, init_ref, o_ref, sem):
    del init_ref
    i = pl.program_id(0)
    tm = x_ref.shape[0]

    def copy(r, dst):
        return pltpu.make_async_copy(x_ref.at[pl.ds(r, 1), :], o_ref.at[pl.ds(dst, 1), :], sem)

    def start(r, carry):
        for kk in range(TOP_K):
            copy(r, dest_ref[(i * tm + r) * TOP_K + kk]).start()
        return carry

    lax.fori_loop(0, tm, start, 0)

    def wait(r, carry):
        for kk in range(TOP_K):
            copy(0, 0).wait()
        return carry

    lax.fori_loop(0, tm, wait, 0)


def _dispatch(dest_flat, xn, rows_padded):
    n, d = xn.shape
    tm = TM_DISP
    init = jnp.zeros((rows_padded, d), xn.dtype)
    return pl.pallas_call(
        _dispatch_kernel,
        grid_spec=pltpu.PrefetchScalarGridSpec(
            num_scalar_prefetch=1,
            grid=(n // tm,),
            in_specs=[pl.BlockSpec((tm, d), lambda i, dest: (i, 0)),
                      pl.BlockSpec(memory_space=pl.ANY)],
            out_specs=pl.BlockSpec(memory_space=pl.ANY),
            scratch_shapes=[pltpu.SemaphoreType.DMA(())],
        ),
        out_shape=jax.ShapeDtypeStruct((rows_padded, d), xn.dtype),
        input_output_aliases={2: 0},
        compiler_params=_cparams(("arbitrary",)),
        name="dispatch",
    )(dest_flat, xn, init)


def _experts_kernel(be_ref, nu_ref, x_ref, w1_ref, b1_ref, w2_ref, b2_ref, o_ref,
                    w1b_ref, w2b_ref, prev_ref):
    j = pl.program_id(0)
    e = be_ref[j]
    ff = w2_ref.shape[0]

    @pl.when(j == 0)
    def _():
        prev_ref[0] = -1

    @pl.when(j < nu_ref[0])
    def _():
        @pl.when(e != prev_ref[0])
        def _():
            w1b_ref[...] = w1_ref[...].astype(BF16)
            w2b_ref[...] = w2_ref[...].astype(BF16)
            prev_ref[0] = e

        h = _dot(x_ref[...].astype(BF16), w1b_ref[...]) + b1_ref[...]
        gate = jnp.minimum(h[:, :ff], SWIGLU_LIMIT)
        lin = jnp.clip(h[:, ff:], -SWIGLU_LIMIT, SWIGLU_LIMIT)
        act = gate * jax.nn.sigmoid(SWIGLU_ALPHA * gate) * (lin + 1.0)
        o_ref[...] = _dot(act.astype(BF16), w2b_ref[...]) + b2_ref[...]

    @pl.when(j >= nu_ref[0])
    def _():
        o_ref[...] = jnp.zeros(o_ref.shape, F32)


def _experts(block_e, nused, xin, w1, b1, w2, b2):
    rows, d = xin.shape
    ne, _, f2 = w1.shape
    ff = w2.shape[1]
    nblk = rows // MOE_BLOCK
    return pl.pallas_call(
        _experts_kernel,
        grid_spec=pltpu.PrefetchScalarGridSpec(
            num_scalar_prefetch=2,
            grid=(nblk,),
            in_specs=[
                pl.BlockSpec((MOE_BLOCK, d), lambda j, be, nu: (j, 0)),
                pl.BlockSpec((None, d, f2), lambda j, be, nu: (be[j], 0, 0)),
                pl.BlockSpec((None, 1, f2), lambda j, be, nu: (be[j], 0, 0)),
                pl.BlockSpec((None, ff, d), lambda j, be, nu: (be[j], 0, 0)),
                pl.BlockSpec((None, 1, d), lambda j, be, nu: (be[j], 0, 0)),
            ],
            out_specs=pl.BlockSpec((MOE_BLOCK, d), lambda j, be, nu: (j, 0)),
            scratch_shapes=[pltpu.VMEM((d, f2), BF16), pltpu.VMEM((ff, d), BF16),
                            pltpu.SMEM((1,), jnp.int32)],
        ),
        out_shape=jax.ShapeDtypeStruct((rows, d), F32),
        compiler_params=_cparams(("arbitrary",)),
        name="experts",
    )(block_e, nused, xin, w1, b1.reshape(ne, 1, f2), w2, b2.reshape(ne, 1, d))


def _combine_kernel(dest_ref, h_ref, gt_ref, g_ref, yb_ref, o_ref, buf_ref, sem):
    i = pl.program_id(0)
    tm = h_ref.shape[0]

    def copy(src, kk, r):
        return pltpu.make_async_copy(yb_ref.at[pl.ds(src, 1), :], buf_ref.at[kk, pl.ds(r, 1), :], sem)

    def start(r, carry):
        for kk in range(TOP_K):
            copy(dest_ref[(i * tm + r) * TOP_K + kk], kk, r).start()
        return carry

    lax.fori_loop(0, tm, start, 0)

    def wait(r, carry):
        for kk in range(TOP_K):
            copy(0, kk, 0).wait()
        return carry

    lax.fori_loop(0, tm, wait, 0)

    gt = gt_ref[...]
    y = h_ref[...]
    for kk in range(TOP_K):
        y = y + gt[:, kk:kk + 1] * buf_ref[kk]
    o_ref[...] = _rms(y, g_ref[...])


def _combine(dest_flat, h2, gates, final_g, yb):
    n, d = h2.shape
    tm = TM_COMB
    return pl.pallas_call(
        _combine_kernel,
        grid_spec=pltpu.PrefetchScalarGridSpec(
            num_scalar_prefetch=1,
            grid=(n // tm,),
            in_specs=[
                pl.BlockSpec((tm, d), lambda i, dest: (i, 0)),
                pl.BlockSpec((tm, TOP_K), lambda i, dest: (i, 0)),
                pl.BlockSpec((1, d), lambda i, dest: (0, 0)),
                pl.BlockSpec(memory_space=pl.ANY),
            ],
            out_specs=pl.BlockSpec((tm, d), lambda i, dest: (i, 0)),
            scratch_shapes=[pltpu.VMEM((TOP_K, tm, d), F32), pltpu.SemaphoreType.DMA(())],
        ),
        out_shape=jax.ShapeDtypeStruct((n, d), F32),
        compiler_params=_cparams(("arbitrary",)),
        name="combine",
    )(dest_flat, h2, gates, final_g.reshape(1, d), yb)


def kernel(x, mem, norm1_g, w_in, mu_shift, w0, w2, a0, a2, g2, k_k, k_a, r_k, lnx_g, lnx_b,
           lam_q1, lam_k1, lam_q2, lam_k2, subln_g, w_out, norm2_g, mem_norm_g, wq_x, wkv_x, wo_x,
           norm3_g, router_w, router_b, moe_w1, moe_b1, moe_w2, moe_b2, final_g):
    batch, seq, d = x.shape
    n = batch * seq
    width = w0.shape[1]
    shift_cols = mu_shift.shape[1]
    x2 = x.reshape(n, d)

    p, qkv = _inproj(x2, norm1_g[0], w_in[0], shift_cols)
    y_rwkv = _rwkv(p, batch, seq, mu_shift[0], w0[0], w2[0], a0[0], a2[0], g2[0], k_k[0], k_a[0],
                   r_k[0].reshape(-1), lnx_g[0], lnx_b[0])
    y_diff = _diffattn(qkv, batch, seq, lam_q1[0], lam_k1[0], lam_q2[0], lam_k2[0], subln_g[0])
    kv = _memkv(mem.reshape(-1, d), mem_norm_g[0], wkv_x[0])

    h2, xn3, top_i, gates, rank, counts = _mid(
        x2, y_rwkv, y_diff, batch, seq, w_out[0], norm2_g[0], wq_x[0], kv, wo_x[0], norm3_g[0],
        router_w[0], router_b[0])

    counts = counts.reshape(-1)
    padded = ((counts + MOE_BLOCK - 1) // MOE_BLOCK) * MOE_BLOCK
    padded_end = jnp.cumsum(padded)
    padded_off = padded_end - padded
    rows_padded = n * TOP_K + N_EXPERTS * MOE_BLOCK
    nblk = rows_padded // MOE_BLOCK
    block_start = jnp.arange(nblk, dtype=jnp.int32) * MOE_BLOCK
    block_e = jnp.minimum(jnp.sum(block_start[:, None] >= padded_end[None, :], axis=1),
                          N_EXPERTS - 1).astype(jnp.int32)
    nused = (padded_end[-1:] // MOE_BLOCK).astype(jnp.int32)
    dest = (jnp.take(padded_off, top_i) + rank).astype(jnp.int32).reshape(-1)

    xin = _dispatch(dest, xn3, rows_padded)
    yb = _experts(block_e, nused, xin, moe_w1[0], moe_b1[0], moe_w2[0], moe_b2[0])
    out = _combine(dest, h2, gates, final_g, yb)
    return out.reshape(batch, seq, d)
```

```python
import functools
import math

import jax
import jax.numpy as jnp
from jax import lax
from jax.experimental import pallas as pl
from jax.experimental.pallas import tpu as pltpu

F32 = jnp.float32
BF16 = jnp.bfloat16

CHUNK = 64
RWKV_HEAD = 64
DECAY_LORA = 64
AAA_LORA = 64
GATE_LORA = 128
DIFF_HEAD = 64
XATTN_HEADS = 4
N_EXPERTS = 32
TOP_K = 4
SWIGLU_LIMIT = 7.0
SWIGLU_ALPHA = 1.702
MOE_BLOCK = 256
NORM_EPS = 1e-5
GN_EPS = 64e-5
LAMBDA_INIT = 0.8 - 0.6 * math.exp(-0.3 * 0)

VMEM_LIMIT = 56 * 1024 * 1024
TM_PROJ = 512
TT_RWKV = 256
TQ_ATT = 512
TK_ATT = 512
TM_MID = 512
TM_DISP = 512
TM_COMB = 256
DMA_UNROLL = 4


def _cparams(sem):
    return pltpu.CompilerParams(dimension_semantics=sem, vmem_limit_bytes=VMEM_LIMIT)


def _dot(a, b):
    return jnp.dot(a, b, preferred_element_type=F32)


def _dot_nt(a, b):
    return lax.dot_general(a, b, (((1,), (1,)), ((), ())), preferred_element_type=F32)


def _dot_tn(a, b):
    return lax.dot_general(a, b, (((0,), (0,)), ((), ())), preferred_element_type=F32)


def _rms(x, g):
    return x * lax.rsqrt(jnp.mean(x * x, axis=-1, keepdims=True) + NORM_EPS) * g


def _split2(x):
    hi = x.astype(BF16)
    lo = (x - hi.astype(F32)).astype(BF16)
    return hi, lo


def _split3(x):
    hi = x.astype(BF16)
    r1 = x - hi.astype(F32)
    mid = r1.astype(BF16)
    lo = (r1 - mid.astype(F32)).astype(BF16)
    return hi, mid, lo


def _inproj_kernel(x_ref, g_ref, wa_ref, wb_ref, wvt_ref, p_ref, qk_ref, vt_ref):
    xn = _rms(x_ref[...], g_ref[...]).astype(BF16)
    p_ref[...] = _dot(xn, wa_ref[...])
    qk_ref[...] = _dot(xn, wb_ref[...]).astype(BF16)
    vt_ref[...] = _dot_nt(wvt_ref[...], xn).astype(BF16)


def _inproj(x2, g, w_in, shift_cols):
    n, d = x2.shape
    dw = (w_in.shape[1] - shift_cols) // 3
    wa = w_in[:, :shift_cols].astype(BF16)
    wb = w_in[:, shift_cols:shift_cols + 2 * dw].astype(BF16)
    wvt = w_in[:, shift_cols + 2 * dw:].T.astype(BF16)
    tm = TM_PROJ
    return pl.pallas_call(
        _inproj_kernel,
        grid=(n // tm,),
        in_specs=[
            pl.BlockSpec((tm, d), lambda i: (i, 0)),
            pl.BlockSpec((1, d), lambda i: (0, 0)),
            pl.BlockSpec((d, shift_cols), lambda i: (0, 0)),
            pl.BlockSpec((d, 2 * dw), lambda i: (0, 0)),
            pl.BlockSpec((dw, d), lambda i: (0, 0)),
        ],
        out_specs=[
            pl.BlockSpec((tm, shift_cols), lambda i: (i, 0)),
            pl.BlockSpec((tm, 2 * dw), lambda i: (i, 0)),
            pl.BlockSpec((dw, tm), lambda i: (0, i)),
        ],
        out_shape=[
            jax.ShapeDtypeStruct((n, shift_cols), F32),
            jax.ShapeDtypeStruct((n, 2 * dw), BF16),
            jax.ShapeDtypeStruct((dw, n), BF16),
        ],
        compiler_params=_cparams(("arbitrary",)),
        name="inproj",
    )(x2, g.reshape(1, d), wa, wb, wvt)


def _rwkv_kernel(p_ref, mu_ref, w0_ref, a0_ref, wwa_ref, g2_ref, kk_ref, ka_ref, rk_ref,
                 lg_ref, lb_ref, seg_ref, tri_ref, blk_ref, o_ref,
                 s_ref, last_ref, ar_ref, bk_ref, bkh_ref, zv_ref, gc_ref, ry_ref, mn_ref, y_ref,
                 bonus_ref, gate_ref, *, width, heads):
    t_idx = pl.program_id(1)
    tt = p_ref.shape[0]
    nchunk = tt // CHUNK
    hd = RWKV_HEAD
    hs = range(heads)

    @pl.when(t_idx == 0)
    def _():
        s_ref[...] = jnp.zeros(s_ref.shape, F32)
        last_ref[...] = jnp.zeros(last_ref.shape, F32)
        zv_ref[...] = jnp.zeros(zv_ref.shape, BF16)

    p = p_ref[...]
    prev = pltpu.roll(p, shift=1, axis=0)
    row0 = lax.broadcasted_iota(jnp.int32, (tt, 1), 0) == 0
    prev = jnp.where(row0, last_ref[0:1, :], prev)
    last_ref[0:1, :] = p[tt - 1:tt, :]
    pm = p + (prev - p) * mu_ref[...]

    r = pm[:, 0:width]
    k = pm[:, width:2 * width]
    v = pm[:, 2 * width:3 * width]
    wa = pm[:, 3 * width:3 * width + DECAY_LORA + AAA_LORA]
    gd = pm[:, 3 * width + DECAY_LORA + AAA_LORA:]

    lane = lax.broadcasted_iota(jnp.int32, wa.shape, 1)
    z = jnp.where(lane < DECAY_LORA, jnp.tanh(wa), wa).astype(BF16)
    lora = _dot(z, wwa_ref[...])
    logw = -math.exp(-0.5) * jax.nn.sigmoid(w0_ref[...] + lora[:, :width])
    lr = jax.nn.sigmoid(a0_ref[...] + lora[:, width:])
    gate_ref[...] = _dot(jax.nn.sigmoid(gd).astype(BF16), g2_ref[...])

    seg = seg_ref[...]

    def segsum(x):
        hi, lo = _split2(x)
        return _dot(hi, seg) + _dot(lo, seg)

    kk = k * kk_ref[...]
    kk = kk / jnp.maximum(jnp.sqrt(segsum(kk * kk)), 1e-12)
    k2 = k * (1.0 + (lr - 1.0) * ka_ref[...])
    bonus_ref[...] = segsum(r * k2 * rk_ref[...]) * v

    h3 = _split3(logw)
    tri = tri_ref[...]
    blk = blk_ref[...]
    cum = _dot(tri, h3[0]) + _dot(tri, h3[1]) + _dot(tri, h3[2])
    tot = _dot(blk, h3[0]) + _dot(blk, h3[1]) + _dot(blk, h3[2])
    einv = jnp.exp(-cum)
    ed = jnp.exp(tot - cum)
    b = kk * lr
    parts = (
        (-kk * jnp.exp(cum - logw), ar_ref, 0), (r * jnp.exp(cum), ar_ref, CHUNK),
        (b * einv, bk_ref, 0), (k2 * einv, bk_ref, CHUNK),
        (b * ed, bkh_ref, 0), (k2 * ed, bkh_ref, CHUNK),
    )
    gc = jnp.exp(tot)
    zlane = jnp.zeros((tt, hd), F32)
    for h in hs:
        sl = slice(h * hd, (h + 1) * hd)
        gc_ref[h] = gc[:, sl]
        zv = jnp.concatenate([zlane, v[:, sl]], axis=1).astype(BF16)
        for c in range(nchunk):
            zv_ref[h, c, CHUNK:, :] = zv[c * CHUNK:(c + 1) * CHUNK]
        for val, ref, off in parts:
            vb = val[:, sl].astype(BF16)
            for c in range(nchunk):
                ref[h, c, off:off + CHUNK, :] = vb[c * CHUNK:(c + 1) * CHUNK]

    ri = lax.broadcasted_iota(jnp.int32, (2 * CHUNK, 2 * CHUNK), 0)
    ci = lax.broadcasted_iota(jnp.int32, (2 * CHUNK, 2 * CHUNK), 1)
    mask = ci % CHUNK < ri % CHUNK + ri // CHUNK
    e_r = lax.broadcasted_iota(jnp.int32, (CHUNK, CHUNK), 0)
    e_c = lax.broadcasted_iota(jnp.int32, (CHUNK, CHUNK), 1)
    eye = (e_r == e_c).astype(F32)
    zpad = jnp.zeros((CHUNK, hd), F32)

    def pre_body(c, carry):
        rows = pl.ds(pl.multiple_of(c * CHUNK, CHUNK), CHUNK)
        ar = [ar_ref[h, c] for h in hs]
        g = [jnp.where(mask, _dot_nt(ar[h], bk_ref[h, c]), 0.0) for h in hs]
        gb = [x.astype(BF16) for x in g]
        gv = [_dot(gb[h], zv_ref[h, c]) for h in hs]
        a = [x[:CHUNK, :hd] for x in g]
        ab = [x.astype(BF16) for x in a]
        pw = [_dot(x, x) for x in ab]
        xs = [eye + x for x in a]
        for _ in range(4):
            res = [_dot(jnp.concatenate([xs[h], pw[h]], axis=0).astype(BF16), pw[h].astype(BF16))
                   for h in hs]
            xs = [xs[h] + res[h][:CHUNK] for h in hs]
            pw = [res[h][CHUNK:] for h in hs]
        tinv = [xs[h] + _dot(xs[h].astype(BF16), pw[h].astype(BF16)) for h in hs]
        e = [jnp.concatenate([ar[h][:CHUNK].astype(F32), zpad], axis=1) + gv[h][:CHUNK] for h in hs]
        wb = [_dot(tinv[h].astype(BF16), e[h].astype(BF16)).astype(BF16) for h in hs]
        for h in hs:
            ry_ref[h, rows, :] = (jnp.concatenate([ar[h][CHUNK:].astype(F32), zpad], axis=1)
                                  + gv[h][CHUNK:] + _dot(gb[h][CHUNK:, :hd], wb[h]))
            left = jnp.concatenate([wb[h], zv_ref[h, c, CHUNK:, :]], axis=0)
            mn_ref[h, c] = _dot_tn(left, bkh_ref[h, c])
        return carry

    lax.fori_loop(0, nchunk, pre_body, 0)

    def seq_body(c, carry):
        r0 = pl.multiple_of(c * CHUNK, CHUNK)
        rows = pl.ds(r0, CHUNK)
        s0 = [s_ref[h] for h in hs]
        s0b = [x.astype(BF16) for x in s0]
        ry = [ry_ref[h, rows, :] for h in hs]
        mn = [mn_ref[h, c] for h in hs]
        for h in hs:
            y_ref[h, rows, :] = ry[h] + _dot_nt(ry[h][:, :hd].astype(BF16), s0b[h])
            s_ref[h, CHUNK:, :] = (s0[h][CHUNK:] * gc_ref[h, pl.ds(r0, 1), :]
                                   + _dot(s0b[h][CHUNK:], mn[h][:CHUNK].astype(BF16)) + mn[h][CHUNK:])
        return carry

    lax.fori_loop(0, nchunk, seq_body, 0)

    y = jnp.concatenate([y_ref[h][:, hd:] for h in hs], axis=-1)
    mean = segsum(y) * (1.0 / hd)
    dlt = y - mean
    var = segsum(dlt * dlt) * (1.0 / hd)
    yn = dlt * lax.rsqrt(var + GN_EPS) * lg_ref[...] + lb_ref[...]
    o_ref[...] = ((yn + bonus_ref[...]) * gate_ref[...]).astype(o_ref.dtype)


def _rwkv(p, batch, seq, mu, w0, w2, a0, a2, g2, k_k, k_a, r_k, lnx_g, lnx_b):
    n, cols = p.shape
    width = w0.shape[0]
    heads = width // RWKV_HEAD
    assert RWKV_HEAD == CHUNK
    tt = TT_RWKV
    nt = seq // tt
    nchunk = tt // CHUNK
    wwa = jnp.zeros((DECAY_LORA + AAA_LORA, 2 * width), F32)
    wwa = wwa.at[:DECAY_LORA, :width].set(w2).at[DECAY_LORA:, width:].set(a2).astype(BF16)
    lane_head = jnp.arange(width) // RWKV_HEAD
    seg = (lane_head[:, None] == lane_head[None, :]).astype(BF16)
    tch = jnp.arange(tt) // CHUNK
    same = tch[:, None] == tch[None, :]
    tri = (same & (jnp.arange(tt)[None, :] <= jnp.arange(tt)[:, None])).astype(BF16)
    blk = same.astype(BF16)
    row = lambda a: a.reshape(1, -1)
    vec = lambda w: pl.BlockSpec((1, w), lambda b, t: (0, 0))
    full2 = lambda a: pl.BlockSpec(a.shape, lambda b, t: (0, 0))
    slab = lambda lanes, dt: pltpu.VMEM((heads, nchunk, 2 * CHUNK, lanes), dt)
    kern = functools.partial(_rwkv_kernel, width=width, heads=heads)
    return pl.pallas_call(
        kern,
        grid=(batch, nt),
        in_specs=[
            pl.BlockSpec((tt, cols), lambda b, t: (b * nt + t, 0)),
            vec(cols), vec(width), vec(width), full2(wwa), pl.BlockSpec(g2.shape, lambda b, t: (0, 0)),
            vec(width), vec(width), vec(width), vec(width), vec(width),
            full2(seg), full2(tri), full2(blk),
        ],
        out_specs=pl.BlockSpec((tt, width), lambda b, t: (b * nt + t, 0)),
        out_shape=jax.ShapeDtypeStruct((n, width), BF16),
        scratch_shapes=[
            pltpu.VMEM((heads, 2 * RWKV_HEAD, RWKV_HEAD), F32),
            pltpu.VMEM((8, cols), F32),
            slab(RWKV_HEAD, BF16),
            slab(RWKV_HEAD, BF16),
            slab(RWKV_HEAD, BF16),
            slab(2 * RWKV_HEAD, BF16),
            pltpu.VMEM((heads, tt, RWKV_HEAD), F32),
            pltpu.VMEM((heads, tt, 2 * RWKV_HEAD), F32),
            slab(RWKV_HEAD, F32),
            pltpu.VMEM((heads, tt, 2 * RWKV_HEAD), F32),
            pltpu.VMEM((tt, width), F32), pltpu.VMEM((tt, width), F32),
        ],
        compiler_params=_cparams(("arbitrary", "arbitrary")),
        name="rwkv",
    )(p, row(mu), row(w0), row(a0), wwa, g2.astype(BF16), row(k_k), row(k_a), row(r_k),
      row(lnx_g), row(lnx_b), seg, tri, blk)


def _diff_kernel(lq1_ref, lk1_ref, lq2_ref, lk2_ref, sg_ref, q_ref, k_ref, vt_ref, o_ref,
                 m0_ref, l0_ref, a0_ref, m1_ref, l1_ref, a1_ref):
    i = pl.program_id(2)
    tq = q_ref.shape[0]
    tk = TK_ATT
    d = DIFF_HEAD
    q = q_ref[...] * (d ** -0.5)
    lane = lax.broadcasted_iota(jnp.int32, q.shape, 1)
    zero = jnp.zeros_like(q)
    q0 = jnp.where(lane < d, q, zero)
    q1 = jnp.where(lane >= d, q, zero)

    for m_ref, l_ref, a_ref in ((m0_ref, l0_ref, a0_ref), (m1_ref, l1_ref, a1_ref)):
        m_ref[...] = jnp.full(m_ref.shape, -1e30, F32)
        l_ref[...] = jnp.zeros(l_ref.shape, F32)
        a_ref[...] = jnp.zeros(a_ref.shape, F32)

    def update(st, vt, m_ref, l_ref, a_ref):
        m_old = m_ref[...]
        m_new = jnp.maximum(m_old, jnp.max(st, axis=0, keepdims=True))
        alpha = jnp.exp(m_old - m_new)
        pt = jnp.exp(st - m_new)
        l_ref[...] = alpha * l_ref[...] + jnp.sum(pt, axis=0, keepdims=True)
        a_ref[...] = alpha * a_ref[...] + _dot(vt, pt.astype(BF16))
        m_ref[...] = m_new

    def block(j, masked):
        start = pl.multiple_of(j * tk, tk)
        kb = k_ref[pl.ds(start, tk), :]
        vt = vt_ref[:, pl.ds(start, tk)]
        s0 = _dot_nt(kb, q0)
        s1 = _dot_nt(kb, q1)
        if masked:
            kc = lax.broadcasted_iota(jnp.int32, (tk, tq), 0) // CHUNK
            qc = lax.broadcasted_iota(jnp.int32, (tk, tq), 1) // CHUNK
            allowed = kc <= qc
            s0 = jnp.where(allowed, s0, -1e30)
            s1 = jnp.where(allowed, s1, -1e30)
        update(s0, vt, m0_ref, l0_ref, a0_ref)
        update(s1, vt, m1_ref, l1_ref, a1_ref)

    def body(j, carry):
        block(j, False)
        return carry

    lax.fori_loop(0, i, body, 0)
    block(i, True)

    lam = (jnp.exp(jnp.sum(lq1_ref[...] * lk1_ref[...], axis=-1, keepdims=True))
           - jnp.exp(jnp.sum(lq2_ref[...] * lk2_ref[...], axis=-1, keepdims=True)) + LAMBDA_INIT)
    ot = a0_ref[...] / l0_ref[...] - lam * (a1_ref[...] / l1_ref[...])
    o = _rms(ot.T, sg_ref[...]) * (1.0 - LAMBDA_INIT)
    o_ref[...] = o.astype(o_ref.dtype)


def _diffattn(qk, vt, batch, seq, lq1, lk1, lq2, lk2, subln_g):
    n = qk.shape[0]
    width = qk.shape[1] // 2
    hw = 2 * DIFF_HEAD
    heads = width // hw
    tq = TQ_ATT
    assert TK_ATT == tq and seq % tq == 0
    nq = seq // tq
    vec = lambda w: pl.BlockSpec((1, w), lambda b, h, i: (0, 0))
    row = lambda a: a.reshape(1, -1)
    stat = pltpu.VMEM((1, tq), F32)
    acc = pltpu.VMEM((hw, tq), F32)
    return pl.pallas_call(
        _diff_kernel,
        grid=(batch, heads, nq),
        in_specs=[
            vec(DIFF_HEAD), vec(DIFF_HEAD), vec(DIFF_HEAD), vec(DIFF_HEAD), vec(hw),
            pl.BlockSpec((tq, hw), lambda b, h, i: (b * nq + i, h)),
            pl.BlockSpec((seq, hw), lambda b, h, i: (b, heads + h)),
            pl.BlockSpec((hw, seq), lambda b, h, i: (h, b)),
        ],
        out_specs=pl.BlockSpec((tq, hw), lambda b, h, i: (b * nq + i, h)),
        out_shape=jax.ShapeDtypeStruct((n, width), BF16),
        scratch_shapes=[stat, stat, acc, stat, stat, acc],
        compiler_params=_cparams(("arbitrary", "arbitrary", "arbitrary")),
        name="diffattn",
    )(row(lq1), row(lk1), row(lq2), row(lk2), row(subln_g), qk, qk, vt)


def _memkv_kernel(m_ref, g_ref, w_ref, o_ref):
    o_ref[...] = _dot(_rms(m_ref[...], g_ref[...]).astype(BF16), w_ref[...]).astype(o_ref.dtype)


def _memkv(mem2, g, wkv):
    n, d = mem2.shape
    c = wkv.shape[1]
    tn = 512
    return pl.pallas_call(
        _memkv_kernel,
        grid=(c // tn,),
        in_specs=[
            pl.BlockSpec((n, d), lambda j: (0, 0)),
            pl.BlockSpec((1, d), lambda j: (0, 0)),
            pl.BlockSpec((d, tn), lambda j: (0, j)),
        ],
        out_specs=pl.BlockSpec((n, tn), lambda j: (0, j)),
        out_shape=jax.ShapeDtypeStruct((n, c), BF16),
        compiler_params=_cparams(("arbitrary",)),
        name="memkv",
    )(mem2, g.reshape(1, d), wkv.astype(BF16))


def _mid_kernel(x_ref, yr_ref, yd_ref, wo1_ref, wo2_ref, g2_ref, wq_ref, km_ref, vm_ref, wox_ref,
                g3_ref, rw_ref, rb_ref, tril_ref,
                h_ref, xn_ref, ti_ref, gt_ref, rk_ref, cnt_ref, base_ref):
    step = pl.program_id(0) * pl.num_programs(1) + pl.program_id(1)
    tm, dm = x_ref.shape
    hd = dm // XATTN_HEADS

    @pl.when(step == 0)
    def _():
        base_ref[...] = jnp.zeros(base_ref.shape, F32)

    h1 = x_ref[...] + _dot(yr_ref[...], wo1_ref[...]) + _dot(yd_ref[...], wo2_ref[...])
    hn = _rms(h1, g2_ref[...]).astype(BF16)
    q = (_dot(hn, wq_ref[...]) * (hd ** -0.5)).astype(BF16)
    outs = []
    for hh in range(XATTN_HEADS):
        sl = slice(hh * hd, (hh + 1) * hd)
        s = _dot_nt(q[:, sl], km_ref[:, sl])
        s = s - jnp.max(s, axis=-1, keepdims=True)
        e = jnp.exp(s)
        pr = e / jnp.sum(e, axis=-1, keepdims=True)
        outs.append(_dot(pr.astype(BF16), vm_ref[:, sl]))
    o = jnp.concatenate(outs, axis=-1).astype(BF16)
    h2 = h1 + _dot(o, wox_ref[...])
    h_ref[...] = h2
    xn = _rms(h2, g3_ref[...])
    xn_ref[...] = xn

    xs = _split3(xn)
    w_hi, w_mid, w_lo = rw_ref[0], rw_ref[1], rw_ref[2]
    logits = (_dot(xs[0], w_hi) + (_dot(xs[0], w_mid) + _dot(xs[1], w_hi))
              + (_dot(xs[0], w_lo) + _dot(xs[1], w_mid) + _dot(xs[2], w_hi))) + rb_ref[...]

    ne = logits.shape[1]
    eio = lax.broadcasted_iota(jnp.int32, (tm, ne), 1).astype(F32)
    work = logits
    vals, idxs, sels = [], [], []
    for _ in range(TOP_K):
        mx = jnp.max(work, axis=-1, keepdims=True)
        ix = jnp.min(jnp.where(work == mx, eio, float(ne)), axis=-1, keepdims=True)
        sel = eio == ix
        vals.append(mx)
        idxs.append(ix)
        sels.append(sel)
        work = jnp.where(sel, -jnp.inf, work)
    es = [jnp.exp(vv - vals[0]) for vv in vals]
    den = es[0] + es[1] + es[2] + es[3]
    gates = [ee / den for ee in es]

    assign = jnp.zeros((tm, ne), F32)
    for sel in sels:
        assign = assign + sel.astype(F32)
    before = _dot(tril_ref[...], assign.astype(BF16)) + base_ref[...]
    ranks = [jnp.sum(jnp.where(sel, before, 0.0), axis=-1, keepdims=True) for sel in sels]
    base_new = base_ref[...] + jnp.sum(assign, axis=0, keepdims=True)
    base_ref[...] = base_new
    cnt_ref[...] = base_new.astype(jnp.int32)

    kio = lax.broadcasted_iota(jnp.int32, (tm, TOP_K), 1)

    def cols(parts):
        out = jnp.where(kio == 0, parts[0], parts[1])
        for kk in range(2, TOP_K):
            out = jnp.where(kio == kk, parts[kk], out)
        return out

    ti_ref[...] = cols(idxs).astype(jnp.int32)
    gt_ref[...] = cols(gates)
    rk_ref[...] = cols(ranks).astype(jnp.int32)


def _mid(x2, yr, yd, batch, seq, w_out, norm2_g, wq, kv, wo, norm3_g, router_w, router_b):
    n, d = x2.shape
    tm = TM_MID
    nt = seq // tm
    mem_len = kv.shape[0] // batch
    wr = yr.shape[1]
    wo1 = w_out[:wr].astype(BF16)
    wo2 = w_out[wr:].astype(BF16)
    rw = jnp.stack(_split3(router_w))
    ne = router_w.shape[1]
    tril = (jnp.arange(tm)[None, :] < jnp.arange(tm)[:, None]).astype(BF16)
    tok = lambda w: pl.BlockSpec((tm, w), lambda b, t: (b * nt + t, 0))
    cst = lambda a: pl.BlockSpec(a.shape, lambda b, t: (0,) * a.ndim)
    row = lambda a: a.reshape(1, -1)
    return pl.pallas_call(
        _mid_kernel,
        grid=(batch, nt),
        in_specs=[
            tok(d), tok(wr), tok(d - wr), cst(wo1), cst(wo2), cst(row(norm2_g)), cst(wq),
            pl.BlockSpec((mem_len, d), lambda b, t: (b, 0)),
            pl.BlockSpec((mem_len, d), lambda b, t: (b, 1)),
            cst(wo), cst(row(norm3_g)), cst(rw), cst(row(router_b)), cst(tril),
        ],
        out_specs=[tok(d), tok(d), tok(TOP_K), tok(TOP_K), tok(TOP_K),
                   pl.BlockSpec((1, ne), lambda b, t: (0, 0))],
        out_shape=[
            jax.ShapeDtypeStruct((n, d), F32),
            jax.ShapeDtypeStruct((n, d), F32),
            jax.ShapeDtypeStruct((n, TOP_K), jnp.int32),
            jax.ShapeDtypeStruct((n, TOP_K), F32),
            jax.ShapeDtypeStruct((n, TOP_K), jnp.int32),
            jax.ShapeDtypeStruct((1, ne), jnp.int32),
        ],
        scratch_shapes=[pltpu.VMEM((1, ne), F32)],
        compiler_params=_cparams(("arbitrary", "arbitrary")),
        name="mid",
    )(x2, yr, yd, wo1, wo2, row(norm2_g), wq.astype(BF16), kv, kv, wo.astype(BF16),
      row(norm3_g), rw, row(router_b), tril)


def _dispatch_kernel(dest_ref, x_ref, init_ref, o_ref, sem):
    del init_ref
    i = pl.program_id(0)
    tm = x_ref.shape[0]

    def copy(r, dst):
        return pltpu.make_async_copy(x_ref.at[pl.ds(r, 1), :], o_ref.at[pl.ds(dst, 1), :], sem)

    def start(r, carry):
        for kk in range(TOP_K):
            copy(r, dest_ref[(i * tm + r) * TOP_K + kk]).start(priority=kk % 2)
        return carry

    lax.fori_loop(0, tm, start, 0, unroll=DMA_UNROLL)

    def wait(r, carry):
        for kk in range(TOP_K):
            copy(0, 0).wait()
        return carry

    lax.fori_loop(0, tm, wait, 0)


def _dispatch(dest_flat, xn, rows_padded):
    n, d = xn.shape
    tm = TM_DISP
    init = jnp.zeros((rows_padded, d), xn.dtype)
    return pl.pallas_call(
        _dispatch_kernel,
        grid_spec=pltpu.PrefetchScalarGridSpec(
            num_scalar_prefetch=1,
            grid=(n // tm,),
            in_specs=[pl.BlockSpec((tm, d), lambda i, dest: (i, 0)),
                      pl.BlockSpec(memory_space=pl.ANY)],
            out_specs=pl.BlockSpec(memory_space=pl.ANY),
            scratch_shapes=[pltpu.SemaphoreType.DMA(())],
        ),
        out_shape=jax.ShapeDtypeStruct((rows_padded, d), xn.dtype),
        input_output_aliases={2: 0},
        compiler_params=_cparams(("arbitrary",)),
        name="dispatch",
    )(dest_flat, xn, init)


def _experts_kernel(be_ref, nu_ref, x_ref, w1_ref, b1_ref, w2_ref, b2_ref, o_ref,
                    w1b_ref, w2b_ref, prev_ref):
    j = pl.program_id(0)
    e = be_ref[j]
    ff = w2_ref.shape[0]

    @pl.when(j == 0)
    def _():
        prev_ref[0] = -1

    @pl.when(j < nu_ref[0])
    def _():
        @pl.when(e != prev_ref[0])
        def _():
            w1b_ref[...] = w1_ref[...].astype(BF16)
            w2b_ref[...] = w2_ref[...].astype(BF16)
            prev_ref[0] = e

        h = _dot(x_ref[...].astype(BF16), w1b_ref[...]) + b1_ref[...]
        gate = jnp.minimum(h[:, :ff], SWIGLU_LIMIT)
        lin = jnp.clip(h[:, ff:], -SWIGLU_LIMIT, SWIGLU_LIMIT)
        act = gate * jax.nn.sigmoid(SWIGLU_ALPHA * gate) * (lin + 1.0)
        o_ref[...] = _dot(act.astype(BF16), w2b_ref[...]) + b2_ref[...]

    @pl.when(j >= nu_ref[0])
    def _():
        o_ref[...] = jnp.zeros(o_ref.shape, F32)


def _experts(block_e, nused, xin, w1, b1, w2, b2):
    rows, d = xin.shape
    ne, _, f2 = w1.shape
    ff = w2.shape[1]
    nblk = rows // MOE_BLOCK
    return pl.pallas_call(
        _experts_kernel,
        grid_spec=pltpu.PrefetchScalarGridSpec(
            num_scalar_prefetch=2,
            grid=(nblk,),
            in_specs=[
                pl.BlockSpec((MOE_BLOCK, d), lambda j, be, nu: (j, 0)),
                pl.BlockSpec((None, d, f2), lambda j, be, nu: (be[j], 0, 0)),
                pl.BlockSpec((None, 1, f2), lambda j, be, nu: (be[j], 0, 0)),
                pl.BlockSpec((None, ff, d), lambda j, be, nu: (be[j], 0, 0)),
                pl.BlockSpec((None, 1, d), lambda j, be, nu: (be[j], 0, 0)),
            ],
            out_specs=pl.BlockSpec((MOE_BLOCK, d), lambda j, be, nu: (j, 0)),
            scratch_shapes=[pltpu.VMEM((d, f2), BF16), pltpu.VMEM((ff, d), BF16),
                            pltpu.SMEM((1,), jnp.int32)],
        ),
        out_shape=jax.ShapeDtypeStruct((rows, d), F32),
        compiler_params=_cparams(("arbitrary",)),
        name="experts",
    )(block_e, nused, xin, w1, b1.reshape(ne, 1, f2), w2, b2.reshape(ne, 1, d))


def _combine_kernel(dest_ref, h_ref, gt_ref, g_ref, yb_ref, o_ref, buf_ref, sem):
    i = pl.program_id(0)
    tm = h_ref.shape[0]

    def copy(src, kk, r):
        return pltpu.make_async_copy(yb_ref.at[pl.ds(src, 1), :], buf_ref.at[kk, pl.ds(r, 1), :], sem)

    def start(r, carry):
        for kk in range(TOP_K):
            copy(dest_ref[(i * tm + r) * TOP_K + kk], kk, r).start(priority=kk % 2)
        return carry

    lax.fori_loop(0, tm, start, 0, unroll=DMA_UNROLL)

    def wait(r, carry):
        for kk in range(TOP_K):
            copy(0, kk, 0).wait()
        return carry

    lax.fori_loop(0, tm, wait, 0)

    gt = gt_ref[...]
    y = h_ref[...]
    for kk in range(TOP_K):
        y = y + gt[:, kk:kk + 1] * buf_ref[kk]
    o_ref[...] = _rms(y, g_ref[...])


def _combine(dest_flat, h2, gates, final_g, yb):
    n, d = h2.shape
    tm = TM_COMB
    return pl.pallas_call(
        _combine_kernel,
        grid_spec=pltpu.PrefetchScalarGridSpec(
            num_scalar_prefetch=1,
            grid=(n // tm,),
            in_specs=[
                pl.BlockSpec((tm, d), lambda i, dest: (i, 0)),
                pl.BlockSpec((tm, TOP_K), lambda i, dest: (i, 0)),
                pl.BlockSpec((1, d), lambda i, dest: (0, 0)),
                pl.BlockSpec(memory_space=pl.ANY),
            ],
            out_specs=pl.BlockSpec((tm, d), lambda i, dest: (i, 0)),
            scratch_shapes=[pltpu.VMEM((TOP_K, tm, d), F32), pltpu.SemaphoreType.DMA(())],
        ),
        out_shape=jax.ShapeDtypeStruct((n, d), F32),
        compiler_params=_cparams(("arbitrary",)),
        name="combine",
    )(dest_flat, h2, gates, final_g.reshape(1, d), yb)


def kernel(x, mem, norm1_g, w_in, mu_shift, w0, w2, a0, a2, g2, k_k, k_a, r_k, lnx_g, lnx_b,
           lam_q1, lam_k1, lam_q2, lam_k2, subln_g, w_out, norm2_g, mem_norm_g, wq_x, wkv_x, wo_x,
           norm3_g, router_w, router_b, moe_w1, moe_b1, moe_w2, moe_b2, final_g):
    batch, seq, d = x.shape
    n = batch * seq
    shift_cols = mu_shift.shape[1]
    x2 = x.reshape(n, d)

    p, qk, vt = _inproj(x2, norm1_g[0], w_in[0], shift_cols)
    y_rwkv = _rwkv(p, batch, seq, mu_shift[0], w0[0], w2[0], a0[0], a2[0], g2[0], k_k[0], k_a[0],
                   r_k[0].reshape(-1), lnx_g[0], lnx_b[0])
    y_diff = _diffattn(qk, vt, batch, seq, lam_q1[0], lam_k1[0], lam_q2[0], lam_k2[0], subln_g[0])
    kv = _memkv(mem.reshape(-1, d), mem_norm_g[0], wkv_x[0])

    h2, xn3, top_i, gates, rank, counts = _mid(
        x2, y_rwkv, y_diff, batch, seq, w_out[0], norm2_g[0], wq_x[0], kv, wo_x[0], norm3_g[0],
        router_w[0], router_b[0])

    counts = counts.reshape(-1)
    padded = ((counts + MOE_BLOCK - 1) // MOE_BLOCK) * MOE_BLOCK
    padded_end = jnp.cumsum(padded)
    padded_off = padded_end - padded
    rows_padded = n * TOP_K + N_EXPERTS * MOE_BLOCK
    nblk = rows_padded // MOE_BLOCK
    block_start = jnp.arange(nblk, dtype=jnp.int32) * MOE_BLOCK
    block_e = jnp.minimum(jnp.sum(block_start[:, None] >= padded_end[None, :], axis=1),
                          N_EXPERTS - 1).astype(jnp.int32)
    nused = (padded_end[-1:] // MOE_BLOCK).astype(jnp.int32)
    dest = (jnp.take(padded_off, top_i) + rank).astype(jnp.int32).reshape(-1)

    xin = _dispatch(dest, xn3, rows_padded)
    yb = _experts(block_e, nused, xin, moe_w1[0], moe_b1[0], moe_w2[0], moe_b2[0])
    out = _combine(dest, h2, gates, final_g, yb)
    return out.reshape(batch, seq, d)
```

```python
import functools
import math

import jax
import jax.numpy as jnp
from jax import lax
from jax.experimental import pallas as pl
from jax.experimental.pallas import tpu as pltpu

F32 = jnp.float32
BF16 = jnp.bfloat16

CHUNK = 64
RWKV_HEAD = 64
DECAY_LORA = 64
AAA_LORA = 64
GATE_LORA = 128
DIFF_HEAD = 64
XATTN_HEADS = 4
N_EXPERTS = 32
TOP_K = 4
SWIGLU_LIMIT = 7.0
SWIGLU_ALPHA = 1.702
NORM_EPS = 1e-5
GN_EPS = 64e-5
LAMBDA_INIT = 0.8 - 0.6 * math.exp(-0.3 * 0)

VMEM_LIMIT = 56 * 1024 * 1024
TM_PROJ = 512
TT_RWKV = 256
VT_ROWS = 144
TQ_ATT = 512
TK_ATT = 512
TM_MID = 512
TM_DISP = 512
TM_COMB = 256
MOE_TILE = 512
DMA_UNROLL = 4
COMB_ROWS = 32
WAIT_UNROLL = 32
ROW_TILE = 8
LANES = 128


def _cparams(sem):
    return pltpu.CompilerParams(dimension_semantics=sem, vmem_limit_bytes=VMEM_LIMIT)


def _dot(a, b):
    return jnp.dot(a, b, preferred_element_type=F32)


def _dot_nt(a, b):
    return lax.dot_general(a, b, (((1,), (1,)), ((), ())), preferred_element_type=F32)


def _dot_tn(a, b):
    return lax.dot_general(a, b, (((0,), (0,)), ((), ())), preferred_element_type=F32)


def _rms(x, g):
    return x * lax.rsqrt(jnp.mean(x * x, axis=-1, keepdims=True) + NORM_EPS) * g


def _load_rows(ref, nrows):
    return jnp.concatenate([ref[pl.ds(j, nrows, stride=ROW_TILE), :] for j in range(ROW_TILE)], axis=1)


def _store_rows(ref, val):
    nrows = val.shape[0]
    for j in range(ROW_TILE):
        ref[pl.ds(j, nrows, stride=ROW_TILE), :] = val[:, j * LANES:(j + 1) * LANES]


def _split2(x):
    hi = x.astype(BF16)
    lo = (x - hi.astype(F32)).astype(BF16)
    return hi, lo


def _split3(x):
    hi = x.astype(BF16)
    r1 = x - hi.astype(F32)
    mid = r1.astype(BF16)
    lo = (r1 - mid.astype(F32)).astype(BF16)
    return hi, mid, lo


def _inproj_kernel(x_ref, g_ref, wa_ref, wb_ref, wvt_ref, p_ref, qk_ref, vt_ref):
    xn = _rms(x_ref[...], g_ref[...]).astype(BF16)
    p_ref[...] = _dot(xn, wa_ref[...])
    qk_ref[...] = _dot(xn, wb_ref[...]).astype(BF16)
    vt = _dot_nt(wvt_ref[...], xn)
    hw = 2 * DIFF_HEAD
    ones = jnp.ones((VT_ROWS - hw, vt.shape[1]), F32)
    for h in range(vt.shape[0] // hw):
        vt_ref[h * VT_ROWS:(h + 1) * VT_ROWS, :] = jnp.concatenate(
            [vt[h * hw:(h + 1) * hw], ones], axis=0).astype(BF16)


def _inproj(x2, g, w_in, shift_cols):
    n, d = x2.shape
    dw = (w_in.shape[1] - shift_cols) // 3
    wa = w_in[:, :shift_cols].astype(BF16)
    wb = w_in[:, shift_cols:shift_cols + 2 * dw].astype(BF16)
    wvt = w_in[:, shift_cols + 2 * dw:].T.astype(BF16)
    tm = TM_PROJ
    return pl.pallas_call(
        _inproj_kernel,
        grid=(n // tm,),
        in_specs=[
            pl.BlockSpec((tm, d), lambda i: (i, 0)),
            pl.BlockSpec((1, d), lambda i: (0, 0)),
            pl.BlockSpec((d, shift_cols), lambda i: (0, 0)),
            pl.BlockSpec((d, 2 * dw), lambda i: (0, 0)),
            pl.BlockSpec((dw, d), lambda i: (0, 0)),
        ],
        out_specs=[
            pl.BlockSpec((tm, shift_cols), lambda i: (i, 0)),
            pl.BlockSpec((tm, 2 * dw), lambda i: (i, 0)),
            pl.BlockSpec((dw // (2 * DIFF_HEAD) * VT_ROWS, tm), lambda i: (0, i)),
        ],
        out_shape=[
            jax.ShapeDtypeStruct((n, shift_cols), F32),
            jax.ShapeDtypeStruct((n, 2 * dw), BF16),
            jax.ShapeDtypeStruct((dw // (2 * DIFF_HEAD) * VT_ROWS, n), BF16),
        ],
        compiler_params=_cparams(("arbitrary",)),
        name="inproj",
    )(x2, g.reshape(1, d), wa, wb, wvt)


def _rwkv_kernel(p_ref, mu_ref, w0_ref, a0_ref, wwa_ref, g2_ref, kk_ref, ka_ref, rk_ref,
                 lg_ref, lb_ref, seg_ref, tri_ref, blk_ref, o_ref,
                 s_ref, last_ref, ar_ref, bk_ref, bkh_ref, zv_ref, gc_ref, ry_ref, mn_ref, y_ref,
                 bonus_ref, gate_ref, *, width, heads):
    t_idx = pl.program_id(1)
    tt = p_ref.shape[0]
    nchunk = tt // CHUNK
    hd = RWKV_HEAD
    hs = range(heads)

    @pl.when(t_idx == 0)
    def _():
        s_ref[...] = jnp.zeros(s_ref.shape, F32)
        last_ref[...] = jnp.zeros(last_ref.shape, F32)
        zv_ref[...] = jnp.zeros(zv_ref.shape, BF16)

    p = p_ref[...]
    prev = pltpu.roll(p, shift=1, axis=0)
    row0 = lax.broadcasted_iota(jnp.int32, (tt, 1), 0) == 0
    prev = jnp.where(row0, last_ref[0:1, :], prev)
    last_ref[0:1, :] = p[tt - 1:tt, :]
    pm = p + (prev - p) * mu_ref[...]

    r = pm[:, 0:width]
    k = pm[:, width:2 * width]
    v = pm[:, 2 * width:3 * width]
    wa = pm[:, 3 * width:3 * width + DECAY_LORA + AAA_LORA]
    gd = pm[:, 3 * width + DECAY_LORA + AAA_LORA:]

    lane = lax.broadcasted_iota(jnp.int32, wa.shape, 1)
    z = jnp.where(lane < DECAY_LORA, jnp.tanh(wa), wa).astype(BF16)
    lora = _dot(z, wwa_ref[...])
    logw = -math.exp(-0.5) * jax.nn.sigmoid(w0_ref[...] + lora[:, :width])
    lr = jax.nn.sigmoid(a0_ref[...] + lora[:, width:])
    gate_ref[...] = _dot(jax.nn.sigmoid(gd).astype(BF16), g2_ref[...])

    seg = seg_ref[...]

    def segsum(x):
        hi, lo = _split2(x)
        return _dot(hi, seg) + _dot(lo, seg)

    kk = k * kk_ref[...]
    kk = kk / jnp.maximum(jnp.sqrt(segsum(kk * kk)), 1e-12)
    k2 = k * (1.0 + (lr - 1.0) * ka_ref[...])
    bonus_ref[...] = segsum(r * k2 * rk_ref[...]) * v

    h3 = _split3(logw)
    tri = tri_ref[...]
    blk = blk_ref[...]
    cum = _dot(tri, h3[0]) + _dot(tri, h3[1]) + _dot(tri, h3[2])
    tot = _dot(blk, h3[0]) + _dot(blk, h3[1]) + _dot(blk, h3[2])
    einv = jnp.exp(-cum)
    ed = jnp.exp(tot - cum)
    b = kk * lr
    parts = (
        (-kk * jnp.exp(cum - logw), ar_ref, 0), (r * jnp.exp(cum), ar_ref, CHUNK),
        (b * einv, bk_ref, 0), (k2 * einv, bk_ref, CHUNK),
        (b * ed, bkh_ref, 0), (k2 * ed, bkh_ref, CHUNK),
    )
    gc = jnp.exp(tot)
    zlane = jnp.zeros((tt, hd), F32)
    for h in hs:
        sl = slice(h * hd, (h + 1) * hd)
        gc_ref[h] = gc[:, sl]
        zv = jnp.concatenate([zlane, v[:, sl]], axis=1).astype(BF16)
        for c in range(nchunk):
            zv_ref[h, c, CHUNK:, :] = zv[c * CHUNK:(c + 1) * CHUNK]
        for val, ref, off in parts:
            vb = val[:, sl].astype(BF16)
            for c in range(nchunk):
                ref[h, c, off:off + CHUNK, :] = vb[c * CHUNK:(c + 1) * CHUNK]

    ri = lax.broadcasted_iota(jnp.int32, (2 * CHUNK, 2 * CHUNK), 0)
    ci = lax.broadcasted_iota(jnp.int32, (2 * CHUNK, 2 * CHUNK), 1)
    mask = ci % CHUNK < ri % CHUNK + ri // CHUNK
    e_r = lax.broadcasted_iota(jnp.int32, (CHUNK, CHUNK), 0)
    e_c = lax.broadcasted_iota(jnp.int32, (CHUNK, CHUNK), 1)
    eye = (e_r == e_c).astype(F32)
    zpad = jnp.zeros((CHUNK, hd), F32)

    def pre_body(c, carry):
        rows = pl.ds(pl.multiple_of(c * CHUNK, CHUNK), CHUNK)
        ar = [ar_ref[h, c] for h in hs]
        g = [jnp.where(mask, _dot_nt(ar[h], bk_ref[h, c]), 0.0) for h in hs]
        gb = [x.astype(BF16) for x in g]
        gv = [_dot(gb[h], zv_ref[h, c]) for h in hs]
        a = [x[:CHUNK, :hd] for x in g]
        ab = [x.astype(BF16) for x in a]
        pw = [_dot(x, x) for x in ab]
        xs = [eye + x for x in a]
        for _ in range(4):
            res = [_dot(jnp.concatenate([xs[h], pw[h]], axis=0).astype(BF16), pw[h].astype(BF16))
                   for h in hs]
            xs = [xs[h] + res[h][:CHUNK] for h in hs]
            pw = [res[h][CHUNK:] for h in hs]
        tinv = [xs[h] + _dot(xs[h].astype(BF16), pw[h].astype(BF16)) for h in hs]
        e = [jnp.concatenate([ar[h][:CHUNK].astype(F32), zpad], axis=1) + gv[h][:CHUNK] for h in hs]
        wb = [_dot(tinv[h].astype(BF16), e[h].astype(BF16)).astype(BF16) for h in hs]
        for h in hs:
            ry_ref[h, rows, :] = (jnp.concatenate([ar[h][CHUNK:].astype(F32), zpad], axis=1)
                                  + gv[h][CHUNK:] + _dot(gb[h][CHUNK:, :hd], wb[h]))
            left = jnp.concatenate([wb[h], zv_ref[h, c, CHUNK:, :]], axis=0)
            mn_ref[h, c] = _dot_tn(left, bkh_ref[h, c])
        return carry

    lax.fori_loop(0, nchunk, pre_body, 0)

    def seq_body(c, carry):
        r0 = pl.multiple_of(c * CHUNK, CHUNK)
        rows = pl.ds(r0, CHUNK)
        s0 = [s_ref[h] for h in hs]
        s0b = [x.astype(BF16) for x in s0]
        ry = [ry_ref[h, rows, :] for h in hs]
        mn = [mn_ref[h, c] for h in hs]
        for h in hs:
            y_ref[h, rows, :] = ry[h] + _dot_nt(ry[h][:, :hd].astype(BF16), s0b[h])
            s_ref[h, CHUNK:, :] = (s0[h][CHUNK:] * gc_ref[h, pl.ds(r0, 1), :]
                                   + _dot(s0b[h][CHUNK:], mn[h][:CHUNK].astype(BF16)) + mn[h][CHUNK:])
        return carry

    lax.fori_loop(0, nchunk, seq_body, 0)

    y = jnp.concatenate([y_ref[h][:, hd:] for h in hs], axis=-1)
    mean = segsum(y) * (1.0 / hd)
    dlt = y - mean
    var = segsum(dlt * dlt) * (1.0 / hd)
    yn = dlt * lax.rsqrt(var + GN_EPS) * lg_ref[...] + lb_ref[...]
    o_ref[...] = ((yn + bonus_ref[...]) * gate_ref[...]).astype(o_ref.dtype)


def _rwkv(p, batch, seq, mu, w0, w2, a0, a2, g2, k_k, k_a, r_k, lnx_g, lnx_b):
    n, cols = p.shape
    width = w0.shape[0]
    heads = width // RWKV_HEAD
    assert RWKV_HEAD == CHUNK
    tt = TT_RWKV
    nt = seq // tt
    nchunk = tt // CHUNK
    wwa = jnp.zeros((DECAY_LORA + AAA_LORA, 2 * width), F32)
    wwa = wwa.at[:DECAY_LORA, :width].set(w2).at[DECAY_LORA:, width:].set(a2).astype(BF16)
    lane_head = jnp.arange(width) // RWKV_HEAD
    seg = (lane_head[:, None] == lane_head[None, :]).astype(BF16)
    tch = jnp.arange(tt) // CHUNK
    same = tch[:, None] == tch[None, :]
    tri = (same & (jnp.arange(tt)[None, :] <= jnp.arange(tt)[:, None])).astype(BF16)
    blk = same.astype(BF16)
    row = lambda a: a.reshape(1, -1)
    vec = lambda w: pl.BlockSpec((1, w), lambda b, t: (0, 0))
    full2 = lambda a: pl.BlockSpec(a.shape, lambda b, t: (0, 0))
    slab = lambda lanes, dt: pltpu.VMEM((heads, nchunk, 2 * CHUNK, lanes), dt)
    kern = functools.partial(_rwkv_kernel, width=width, heads=heads)
    return pl.pallas_call(
        kern,
        grid=(batch, nt),
        in_specs=[
            pl.BlockSpec((tt, cols), lambda b, t: (b * nt + t, 0)),
            vec(cols), vec(width), vec(width), full2(wwa), pl.BlockSpec(g2.shape, lambda b, t: (0, 0)),
            vec(width), vec(width), vec(width), vec(width), vec(width),
            full2(seg), full2(tri), full2(blk),
        ],
        out_specs=pl.BlockSpec((tt, width), lambda b, t: (b * nt + t, 0)),
        out_shape=jax.ShapeDtypeStruct((n, width), BF16),
        scratch_shapes=[
            pltpu.VMEM((heads, 2 * RWKV_HEAD, RWKV_HEAD), F32),
            pltpu.VMEM((8, cols), F32),
            slab(RWKV_HEAD, BF16),
            slab(RWKV_HEAD, BF16),
            slab(RWKV_HEAD, BF16),
            slab(2 * RWKV_HEAD, BF16),
            pltpu.VMEM((heads, tt, RWKV_HEAD), F32),
            pltpu.VMEM((heads, tt, 2 * RWKV_HEAD), F32),
            slab(RWKV_HEAD, F32),
            pltpu.VMEM((heads, tt, 2 * RWKV_HEAD), F32),
            pltpu.VMEM((tt, width), F32), pltpu.VMEM((tt, width), F32),
        ],
        compiler_params=_cparams(("arbitrary", "arbitrary")),
        name="rwkv",
    )(p, row(mu), row(w0), row(a0), wwa, g2.astype(BF16), row(k_k), row(k_a), row(r_k),
      row(lnx_g), row(lnx_b), seg, tri, blk)


def _diff_kernel(lq1_ref, lk1_ref, lq2_ref, lk2_ref, sg_ref, q_ref, k_ref, vt_ref, o_ref,
                 m0_ref, a0_ref, m1_ref, a1_ref):
    i = pl.program_id(2)
    tq = q_ref.shape[0]
    tk = TK_ATT
    d = DIFF_HEAD
    hw = 2 * d
    q = (q_ref[...].astype(F32) * (d ** -0.5 * math.log2(math.e))).astype(BF16)
    lane = lax.broadcasted_iota(jnp.int32, q.shape, 1)
    zero = jnp.zeros_like(q)
    q0 = jnp.where(lane < d, q, zero)
    q1 = jnp.where(lane >= d, q, zero)

    for m_ref, a_ref in ((m0_ref, a0_ref), (m1_ref, a1_ref)):
        m_ref[...] = jnp.full(m_ref.shape, -1e30, F32)
        a_ref[...] = jnp.zeros(a_ref.shape, F32)

    def update(st, vt, m_ref, a_ref):
        m_old = m_ref[...]
        m_new = jnp.maximum(m_old, jnp.max(st, axis=0, keepdims=True))
        alpha = jnp.exp2(m_old - m_new)
        pt = jnp.exp2(st - m_new).astype(BF16)
        a_ref[...] = alpha * a_ref[...] + _dot(vt, pt)
        m_ref[...] = m_new

    def block(j, masked):
        start = pl.multiple_of(j * tk, tk)
        kb = k_ref[pl.ds(start, tk), :]
        vt = vt_ref[:, pl.ds(start, tk)]
        s0 = _dot_nt(kb, q0)
        s1 = _dot_nt(kb, q1)
        if masked:
            kc = lax.broadcasted_iota(jnp.int32, (tk, tq), 0) // CHUNK
            qc = lax.broadcasted_iota(jnp.int32, (tk, tq), 1) // CHUNK
            allowed = kc <= qc
            s0 = jnp.where(allowed, s0, -1e30)
            s1 = jnp.where(allowed, s1, -1e30)
        update(s0, vt, m0_ref, a0_ref)
        update(s1, vt, m1_ref, a1_ref)

    def body(j, carry):
        block(j, False)
        return carry

    lax.fori_loop(0, i, body, 0)
    block(i, True)

    lam = (jnp.exp(jnp.sum(lq1_ref[...] * lk1_ref[...], axis=-1, keepdims=True))
           - jnp.exp(jnp.sum(lq2_ref[...] * lk2_ref[...], axis=-1, keepdims=True)) + LAMBDA_INIT)
    ot = (a0_ref[:hw, :] / a0_ref[hw:hw + 1, :] - lam * (a1_ref[:hw, :] / a1_ref[hw:hw + 1, :]))
    o = _rms(ot.T, sg_ref[...]) * (1.0 - LAMBDA_INIT)
    o_ref[...] = o.astype(o_ref.dtype)


def _diffattn(qk, vt, batch, seq, lq1, lk1, lq2, lk2, subln_g):
    n = qk.shape[0]
    width = qk.shape[1] // 2
    hw = 2 * DIFF_HEAD
    heads = width // hw
    tq = TQ_ATT
    assert TK_ATT == tq and seq % tq == 0
    nq = seq // tq
    vec = lambda w: pl.BlockSpec((1, w), lambda b, h, i: (0, 0))
    row = lambda a: a.reshape(1, -1)
    stat = pltpu.VMEM((1, tq), F32)
    acc = pltpu.VMEM((VT_ROWS, tq), F32)
    return pl.pallas_call(
        _diff_kernel,
        grid=(batch, heads, nq),
        in_specs=[
            vec(DIFF_HEAD), vec(DIFF_HEAD), vec(DIFF_HEAD), vec(DIFF_HEAD), vec(hw),
            pl.BlockSpec((tq, hw), lambda b, h, i: (b * nq + i, h)),
            pl.BlockSpec((seq, hw), lambda b, h, i: (b, heads + h)),
            pl.BlockSpec((VT_ROWS, seq), lambda b, h, i: (h, b)),
        ],
        out_specs=pl.BlockSpec((tq, hw), lambda b, h, i: (b * nq + i, h)),
        out_shape=jax.ShapeDtypeStruct((n, width), BF16),
        scratch_shapes=[stat, acc, stat, acc],
        compiler_params=_cparams(("arbitrary", "arbitrary", "arbitrary")),
        name="diffattn",
    )(row(lq1), row(lk1), row(lq2), row(lk2), row(subln_g), qk, qk, vt)


def _memkv_kernel(m_ref, g_ref, w_ref, o_ref):
    o_ref[...] = _dot(_rms(m_ref[...], g_ref[...]).astype(BF16), w_ref[...]).astype(o_ref.dtype)


def _memkv(mem2, g, wkv):
    n, d = mem2.shape
    c = wkv.shape[1]
    tn = 512
    return pl.pallas_call(
        _memkv_kernel,
        grid=(c // tn,),
        in_specs=[
            pl.BlockSpec((n, d), lambda j: (0, 0)),
            pl.BlockSpec((1, d), lambda j: (0, 0)),
            pl.BlockSpec((d, tn), lambda j: (0, j)),
        ],
        out_specs=pl.BlockSpec((n, tn), lambda j: (0, j)),
        out_shape=jax.ShapeDtypeStruct((n, c), BF16),
        compiler_params=_cparams(("arbitrary",)),
        name="memkv",
    )(mem2, g.reshape(1, d), wkv.astype(BF16))


def _mid_kernel(x_ref, yr_ref, yd_ref, wo1_ref, wo2_ref, g2_ref, wq_ref, km_ref, vm_ref, wox_ref,
                g3_ref, rw_ref, rb_ref, tril_ref,
                h_ref, xn_ref, ti_ref, gt_ref, rk_ref, cnt_ref, base_ref):
    step = pl.program_id(0) * pl.num_programs(1) + pl.program_id(1)
    tm, dm = x_ref.shape
    hd = dm // XATTN_HEADS

    @pl.when(step == 0)
    def _():
        base_ref[...] = jnp.zeros(base_ref.shape, F32)

    h1 = x_ref[...] + _dot(yr_ref[...], wo1_ref[...]) + _dot(yd_ref[...], wo2_ref[...])
    hn = _rms(h1, g2_ref[...]).astype(BF16)
    q = (_dot(hn, wq_ref[...]) * (hd ** -0.5)).astype(BF16)
    outs = []
    for hh in range(XATTN_HEADS):
        sl = slice(hh * hd, (hh + 1) * hd)
        s = _dot_nt(q[:, sl], km_ref[:, sl])
        s = s - jnp.max(s, axis=-1, keepdims=True)
        e = jnp.exp(s)
        pr = e / jnp.sum(e, axis=-1, keepdims=True)
        outs.append(_dot(pr.astype(BF16), vm_ref[:, sl]))
    o = jnp.concatenate(outs, axis=-1).astype(BF16)
    h2 = h1 + _dot(o, wox_ref[...])
    h_ref[...] = h2
    xn = _rms(h2, g3_ref[...])
    _store_rows(xn_ref, xn)

    xs = _split3(xn)
    w_hi, w_mid, w_lo = rw_ref[0], rw_ref[1], rw_ref[2]
    logits = (_dot(xs[0], w_hi) + (_dot(xs[0], w_mid) + _dot(xs[1], w_hi))
              + (_dot(xs[0], w_lo) + _dot(xs[1], w_mid) + _dot(xs[2], w_hi))) + rb_ref[...]

    ne = logits.shape[1]
    eio = lax.broadcasted_iota(jnp.int32, (tm, ne), 1).astype(F32)
    work = logits
    vals, idxs, sels = [], [], []
    for _ in range(TOP_K):
        mx = jnp.max(work, axis=-1, keepdims=True)
        ix = jnp.min(jnp.where(work == mx, eio, float(ne)), axis=-1, keepdims=True)
        sel = eio == ix
        vals.append(mx)
        idxs.append(ix)
        sels.append(sel)
        work = jnp.where(sel, -jnp.inf, work)
    es = [jnp.exp(vv - vals[0]) for vv in vals]
    den = es[0] + es[1] + es[2] + es[3]
    gates = [ee / den for ee in es]

    assign = jnp.zeros((tm, ne), F32)
    for sel in sels:
        assign = assign + sel.astype(F32)
    before = _dot(tril_ref[...], assign.astype(BF16)) + base_ref[...]
    ranks = [jnp.sum(jnp.where(sel, before, 0.0), axis=-1, keepdims=True) for sel in sels]
    base_new = base_ref[...] + jnp.sum(assign, axis=0, keepdims=True)
    base_ref[...] = base_new
    cnt_ref[...] = base_new.astype(jnp.int32)

    kio = lax.broadcasted_iota(jnp.int32, (tm, TOP_K), 1)

    def cols(parts):
        out = jnp.where(kio == 0, parts[0], parts[1])
        for kk in range(2, TOP_K):
            out = jnp.where(kio == kk, parts[kk], out)
        return out

    ti_ref[...] = cols(idxs).astype(jnp.int32)
    gt_ref[...] = cols(gates)
    rk_ref[...] = cols(ranks).astype(jnp.int32)


def _mid(x2, yr, yd, batch, seq, w_out, norm2_g, wq, kv, wo, norm3_g, router_w, router_b):
    n, d = x2.shape
    tm = TM_MID
    nt = seq // tm
    mem_len = kv.shape[0] // batch
    wr = yr.shape[1]
    wo1 = w_out[:wr].astype(BF16)
    wo2 = w_out[wr:].astype(BF16)
    rw = jnp.stack(_split3(router_w))
    ne = router_w.shape[1]
    tril = (jnp.arange(tm)[None, :] < jnp.arange(tm)[:, None]).astype(BF16)
    tok = lambda w: pl.BlockSpec((tm, w), lambda b, t: (b * nt + t, 0))
    cst = lambda a: pl.BlockSpec(a.shape, lambda b, t: (0,) * a.ndim)
    row = lambda a: a.reshape(1, -1)
    return pl.pallas_call(
        _mid_kernel,
        grid=(batch, nt),
        in_specs=[
            tok(d), tok(wr), tok(d - wr), cst(wo1), cst(wo2), cst(row(norm2_g)), cst(wq),
            pl.BlockSpec((mem_len, d), lambda b, t: (b, 0)),
            pl.BlockSpec((mem_len, d), lambda b, t: (b, 1)),
            cst(wo), cst(row(norm3_g)), cst(rw), cst(row(router_b)), cst(tril),
        ],
        out_specs=[tok(d), pl.BlockSpec((tm * ROW_TILE, LANES), lambda b, t: (b * nt + t, 0)),
                   tok(TOP_K), tok(TOP_K), tok(TOP_K),
                   pl.BlockSpec((1, ne), lambda b, t: (0, 0))],
        out_shape=[
            jax.ShapeDtypeStruct((n, d), F32),
            jax.ShapeDtypeStruct((n * ROW_TILE, LANES), F32),
            jax.ShapeDtypeStruct((n, TOP_K), jnp.int32),
            jax.ShapeDtypeStruct((n, TOP_K), F32),
            jax.ShapeDtypeStruct((n, TOP_K), jnp.int32),
            jax.ShapeDtypeStruct((1, ne), jnp.int32),
        ],
        scratch_shapes=[pltpu.VMEM((1, ne), F32)],
        compiler_params=_cparams(("arbitrary", "arbitrary")),
        name="mid",
    )(x2, yr, yd, wo1, wo2, row(norm2_g), wq.astype(BF16), kv, kv, wo.astype(BF16),
      row(norm3_g), rw, row(router_b), tril)


def _dispatch_kernel(dest_ref, pad_start_ref, pad_len_ref, x_ref, o_ref, zero_ref, sem, zsem):
    i = pl.program_id(0)
    tm = x_ref.shape[0] // ROW_TILE

    @pl.when(i == 0)
    def _():
        zero_ref[...] = jnp.zeros(zero_ref.shape, zero_ref.dtype)
        pieces = [MOE_TILE >> (s + 1) for s in range(MOE_TILE.bit_length() - 1)]

        def zcopy(nrows, dst_row):
            return pltpu.make_async_copy(
                zero_ref.at[pl.ds(0, nrows * ROW_TILE), :],
                o_ref.at[pl.ds(pl.multiple_of(dst_row * ROW_TILE, ROW_TILE), nrows * ROW_TILE), :], zsem)

        def per_expert(e, carry):
            row = pad_start_ref[e]
            left = pad_len_ref[e]
            for piece in pieces:
                hit = (left & piece) != 0

                @pl.when(hit)
                def _():
                    zcopy(piece, row).start()

                row = row + jnp.where(hit, piece, 0)
            row = pad_start_ref[e]
            for piece in pieces:
                hit = (left & piece) != 0

                @pl.when(hit)
                def _():
                    zcopy(piece, row).wait()

                row = row + jnp.where(hit, piece, 0)
            return carry

        lax.fori_loop(0, N_EXPERTS, per_expert, 0)

        half = MOE_TILE // 2
        tail_start = pad_start_ref[N_EXPERTS]
        tail_pieces = pad_len_ref[N_EXPERTS] // half

        def tail_start_copy(t, carry):
            zcopy(half, tail_start + t * half).start()
            return carry

        def tail_wait_copy(t, carry):
            zcopy(half, tail_start).wait()
            return carry

        lax.fori_loop(0, tail_pieces, tail_start_copy, 0)
        lax.fori_loop(0, tail_pieces, tail_wait_copy, 0)

    def copy(r8, dst8):
        return pltpu.make_async_copy(x_ref.at[pl.ds(pl.multiple_of(r8, ROW_TILE), ROW_TILE), :],
                                     o_ref.at[pl.ds(pl.multiple_of(dst8, ROW_TILE), ROW_TILE), :], sem)

    def start(r, carry):
        for kk in range(TOP_K):
            copy(r * ROW_TILE, dest_ref[(i * tm + r) * TOP_K + kk]).start(priority=kk % 2)
        return carry

    lax.fori_loop(0, tm, start, 0, unroll=DMA_UNROLL)

    def wait(r, carry):
        for kk in range(TOP_K):
            copy(0, 0).wait()
        return carry

    lax.fori_loop(0, tm, wait, 0, unroll=WAIT_UNROLL)


def _dispatch(dest8, pad_start, pad_len, xn, rows_padded):
    tm = TM_DISP
    n = xn.shape[0] // ROW_TILE
    return pl.pallas_call(
        _dispatch_kernel,
        grid_spec=pltpu.PrefetchScalarGridSpec(
            num_scalar_prefetch=3,
            grid=(n // tm,),
            in_specs=[pl.BlockSpec((tm * ROW_TILE, LANES), lambda i, *_: (i, 0))],
            out_specs=pl.BlockSpec(memory_space=pl.ANY),
            scratch_shapes=[pltpu.VMEM((MOE_TILE // 2 * ROW_TILE, LANES), xn.dtype),
                            pltpu.SemaphoreType.DMA(()), pltpu.SemaphoreType.DMA(())],
        ),
        out_shape=jax.ShapeDtypeStruct((rows_padded * ROW_TILE, LANES), xn.dtype),
        compiler_params=_cparams(("arbitrary",)),
        name="dispatch",
    )(dest8, pad_start, pad_len, xn)


def _experts_kernel(be_ref, nu_ref, x_ref, w1_ref, b1_ref, w2_ref, b2_ref, o_ref,
                    w1b_ref, w2b_ref, prev_ref):
    j = pl.program_id(0)
    e = be_ref[j]
    ff = w2_ref.shape[0]

    @pl.when(j == 0)
    def _():
        prev_ref[0] = -1

    @pl.when(j < nu_ref[0])
    def _():
        @pl.when(e != prev_ref[0])
        def _():
            w1b_ref[...] = w1_ref[...].astype(BF16)
            w2b_ref[...] = w2_ref[...].astype(BF16)
            prev_ref[0] = e

        h = _dot(_load_rows(x_ref, MOE_TILE).astype(BF16), w1b_ref[...]) + b1_ref[...]
        gate = jnp.minimum(h[:, :ff], SWIGLU_LIMIT)
        lin = jnp.clip(h[:, ff:], -SWIGLU_LIMIT, SWIGLU_LIMIT)
        act = gate * jax.nn.sigmoid(SWIGLU_ALPHA * gate) * (lin + 1.0)
        _store_rows(o_ref, _dot(act.astype(BF16), w2b_ref[...]) + b2_ref[...])

    @pl.when(j >= nu_ref[0])
    def _():
        o_ref[...] = jnp.zeros(o_ref.shape, F32)


def _experts(block_e, nused, xin, w1, b1, w2, b2):
    ne, d, f2 = w1.shape
    rows = xin.shape[0] // ROW_TILE
    ff = w2.shape[1]
    nblk = rows // MOE_TILE
    return pl.pallas_call(
        _experts_kernel,
        grid_spec=pltpu.PrefetchScalarGridSpec(
            num_scalar_prefetch=2,
            grid=(nblk,),
            in_specs=[
                pl.BlockSpec((MOE_TILE * ROW_TILE, LANES), lambda j, be, nu: (jnp.minimum(j, nu[0] - 1), 0)),
                pl.BlockSpec((None, d, f2), lambda j, be, nu: (be[j], 0, 0)),
                pl.BlockSpec((None, 1, f2), lambda j, be, nu: (be[j], 0, 0)),
                pl.BlockSpec((None, ff, d), lambda j, be, nu: (be[j], 0, 0)),
                pl.BlockSpec((None, 1, d), lambda j, be, nu: (be[j], 0, 0)),
            ],
            out_specs=pl.BlockSpec((MOE_TILE * ROW_TILE, LANES), lambda j, be, nu: (j, 0)),
            scratch_shapes=[pltpu.VMEM((d, f2), BF16), pltpu.VMEM((ff, d), BF16),
                            pltpu.SMEM((1,), jnp.int32)],
        ),
        out_shape=jax.ShapeDtypeStruct(xin.shape, F32),
        compiler_params=_cparams(("arbitrary",)),
        name="experts",
    )(block_e, nused, xin, w1, b1.reshape(ne, 1, f2), w2, b2.reshape(ne, 1, d))


def _combine_kernel(dest_ref, h_ref, gt_ref, g_ref, yb_ref, o_ref, buf_ref, sem):
    i = pl.program_id(0)
    tm = h_ref.shape[0]

    def copy(src8, kk, r8):
        return pltpu.make_async_copy(yb_ref.at[pl.ds(pl.multiple_of(src8, ROW_TILE), ROW_TILE), :],
                                     buf_ref.at[kk, pl.ds(pl.multiple_of(r8, ROW_TILE), ROW_TILE), :], sem)

    def start(r, carry):
        for kk in range(TOP_K):
            copy(dest_ref[(i * tm + r) * TOP_K + kk], kk, r * ROW_TILE).start(priority=kk % 2)
        return carry

    lax.fori_loop(0, tm, start, 0, unroll=DMA_UNROLL)

    def wait(r, carry):
        for kk in range(TOP_K):
            copy(0, kk, 0).wait()
        return carry

    lax.fori_loop(0, tm, wait, 0, unroll=WAIT_UNROLL)

    gt = gt_ref[...]
    rg = COMB_ROWS
    for g0 in range(0, tm, rg):
        rows = slice(g0, g0 + rg)
        parts = []
        for j in range(ROW_TILE):
            y = h_ref[rows, j * LANES:(j + 1) * LANES]
            for kk in range(TOP_K):
                y = y + gt[rows, kk:kk + 1] * buf_ref[kk, pl.ds(g0 * ROW_TILE + j, rg, stride=ROW_TILE), :]
            parts.append(y)
        y = jnp.concatenate(parts, axis=1)
        o_ref[rows, :] = _rms(y, g_ref[...])


def _combine(dest_flat, h2, gates, final_g, yb):
    n, d = h2.shape
    tm = TM_COMB
    return pl.pallas_call(
        _combine_kernel,
        grid_spec=pltpu.PrefetchScalarGridSpec(
            num_scalar_prefetch=1,
            grid=(n // tm,),
            in_specs=[
                pl.BlockSpec((tm, d), lambda i, dest: (i, 0)),
                pl.BlockSpec((tm, TOP_K), lambda i, dest: (i, 0)),
                pl.BlockSpec((1, d), lambda i, dest: (0, 0)),
                pl.BlockSpec(memory_space=pl.ANY),
            ],
            out_specs=pl.BlockSpec((tm, d), lambda i, dest: (i, 0)),
            scratch_shapes=[pltpu.VMEM((TOP_K, tm * ROW_TILE, LANES), F32), pltpu.SemaphoreType.DMA(())],
        ),
        out_shape=jax.ShapeDtypeStruct((n, d), F32),
        compiler_params=_cparams(("arbitrary",)),
        name="combine",
    )(dest_flat, h2, gates, final_g.reshape(1, d), yb)


def kernel(x, mem, norm1_g, w_in, mu_shift, w0, w2, a0, a2, g2, k_k, k_a, r_k, lnx_g, lnx_b,
           lam_q1, lam_k1, lam_q2, lam_k2, subln_g, w_out, norm2_g, mem_norm_g, wq_x, wkv_x, wo_x,
           norm3_g, router_w, router_b, moe_w1, moe_b1, moe_w2, moe_b2, final_g):
    batch, seq, d = x.shape
    n = batch * seq
    shift_cols = mu_shift.shape[1]
    x2 = x.reshape(n, d)

    p, qk, vt = _inproj(x2, norm1_g[0], w_in[0], shift_cols)
    y_rwkv = _rwkv(p, batch, seq, mu_shift[0], w0[0], w2[0], a0[0], a2[0], g2[0], k_k[0], k_a[0],
                   r_k[0].reshape(-1), lnx_g[0], lnx_b[0])
    y_diff = _diffattn(qk, vt, batch, seq, lam_q1[0], lam_k1[0], lam_q2[0], lam_k2[0], subln_g[0])
    kv = _memkv(mem.reshape(-1, d), mem_norm_g[0], wkv_x[0])

    h2, xn3, top_i, gates, rank, counts = _mid(
        x2, y_rwkv, y_diff, batch, seq, w_out[0], norm2_g[0], wq_x[0], kv, wo_x[0], norm3_g[0],
        router_w[0], router_b[0])

    counts = counts.reshape(-1)
    padded = ((counts + MOE_TILE - 1) // MOE_TILE) * MOE_TILE
    padded_end = jnp.cumsum(padded)
    padded_off = padded_end - padded
    rows_padded = n * TOP_K + N_EXPERTS * MOE_TILE
    nblk = rows_padded // MOE_TILE
    block_start = jnp.arange(nblk, dtype=jnp.int32) * MOE_TILE
    block_e = jnp.minimum(jnp.sum(block_start[:, None] >= padded_end[None, :], axis=1),
                          N_EXPERTS - 1).astype(jnp.int32)
    nused = (padded_end[-1:] // MOE_TILE).astype(jnp.int32)
    dest8 = ((jnp.take(padded_off, top_i) + rank) * ROW_TILE).astype(jnp.int32).reshape(-1)
    pad_start = jnp.concatenate([padded_off + counts, padded_end[-1:]]).astype(jnp.int32)
    pad_len = jnp.concatenate([padded - counts, rows_padded - padded_end[-1:]]).astype(jnp.int32)

    xin = _dispatch(dest8, pad_start, pad_len, xn3, rows_padded)
    yb = _experts(block_e, nused, xin, moe_w1[0], moe_b1[0], moe_w2[0], moe_b2[0])
    out = _combine(dest8, h2, gates, final_g, yb)
    return out.reshape(batch, seq, d)
```

```python
import functools
import math

import jax
import jax.numpy as jnp
from jax import lax
from jax.experimental import pallas as pl
from jax.experimental.pallas import tpu as pltpu

F32 = jnp.float32
BF16 = jnp.bfloat16

CHUNK = 64
RWKV_HEAD = 64
DECAY_LORA = 64
AAA_LORA = 64
GATE_LORA = 128
DIFF_HEAD = 64
XATTN_HEADS = 4
N_EXPERTS = 32
TOP_K = 4
SWIGLU_LIMIT = 7.0
SWIGLU_ALPHA = 1.702
NORM_EPS = 1e-5
GN_EPS = 64e-5
LAMBDA_INIT = 0.8 - 0.6 * math.exp(-0.3 * 0)

VMEM_LIMIT = 56 * 1024 * 1024
TM_PROJ = 512
TT_RWKV = 256
VT_ROWS = 144
TQ_ATT = 512
TK_ATT = 512
TM_MID = 512
TM_DISP = 512
TM_COMB = 512
MOE_TILE = 512
DMA_UNROLL = 4
COMB_ROWS = 32
WAIT_UNROLL = 32
ROW_TILE = 8
LANES = 128


def _cparams(sem):
    return pltpu.CompilerParams(dimension_semantics=sem, vmem_limit_bytes=VMEM_LIMIT)


def _dot(a, b):
    return jnp.dot(a, b, preferred_element_type=F32)


def _dot_nt(a, b):
    return lax.dot_general(a, b, (((1,), (1,)), ((), ())), preferred_element_type=F32)


def _dot_tn(a, b):
    return lax.dot_general(a, b, (((0,), (0,)), ((), ())), preferred_element_type=F32)


def _rms(x, g):
    return x * lax.rsqrt(jnp.mean(x * x, axis=-1, keepdims=True) + NORM_EPS) * g


def _load_rows(ref, nrows):
    return jnp.concatenate([ref[pl.ds(j, nrows, stride=ROW_TILE), :] for j in range(ROW_TILE)], axis=1)


def _store_rows(ref, val):
    nrows = val.shape[0]
    for j in range(ROW_TILE):
        ref[pl.ds(j, nrows, stride=ROW_TILE), :] = val[:, j * LANES:(j + 1) * LANES]


def _split2(x):
    hi = x.astype(BF16)
    lo = (x - hi.astype(F32)).astype(BF16)
    return hi, lo


def _split3(x):
    hi = x.astype(BF16)
    r1 = x - hi.astype(F32)
    mid = r1.astype(BF16)
    lo = (r1 - mid.astype(F32)).astype(BF16)
    return hi, mid, lo


def _inproj_kernel(x_ref, g_ref, wa_ref, wb_ref, wvt_ref, p_ref, qk_ref, vt_ref):
    xn = _rms(x_ref[...], g_ref[...]).astype(BF16)
    p_ref[...] = _dot(xn, wa_ref[...])
    qk_ref[...] = _dot(xn, wb_ref[...]).astype(BF16)
    vt = _dot_nt(wvt_ref[...], xn)
    hw = 2 * DIFF_HEAD
    ones = jnp.ones((VT_ROWS - hw, vt.shape[1]), F32)
    for h in range(vt.shape[0] // hw):
        vt_ref[h * VT_ROWS:(h + 1) * VT_ROWS, :] = jnp.concatenate(
            [vt[h * hw:(h + 1) * hw], ones], axis=0).astype(BF16)


def _inproj(x2, g, w_in, shift_cols):
    n, d = x2.shape
    dw = (w_in.shape[1] - shift_cols) // 3
    wa = w_in[:, :shift_cols].astype(BF16)
    wb = w_in[:, shift_cols:shift_cols + 2 * dw].astype(BF16)
    wvt = w_in[:, shift_cols + 2 * dw:].T.astype(BF16)
    tm = TM_PROJ
    return pl.pallas_call(
        _inproj_kernel,
        grid=(n // tm,),
        in_specs=[
            pl.BlockSpec((tm, d), lambda i: (i, 0)),
            pl.BlockSpec((1, d), lambda i: (0, 0)),
            pl.BlockSpec((d, shift_cols), lambda i: (0, 0)),
            pl.BlockSpec((d, 2 * dw), lambda i: (0, 0)),
            pl.BlockSpec((dw, d), lambda i: (0, 0)),
        ],
        out_specs=[
            pl.BlockSpec((tm, shift_cols), lambda i: (i, 0)),
            pl.BlockSpec((tm, 2 * dw), lambda i: (i, 0)),
            pl.BlockSpec((dw // (2 * DIFF_HEAD) * VT_ROWS, tm), lambda i: (0, i)),
        ],
        out_shape=[
            jax.ShapeDtypeStruct((n, shift_cols), F32),
            jax.ShapeDtypeStruct((n, 2 * dw), BF16),
            jax.ShapeDtypeStruct((dw // (2 * DIFF_HEAD) * VT_ROWS, n), BF16),
        ],
        compiler_params=_cparams(("arbitrary",)),
        name="inproj",
    )(x2, g.reshape(1, d), wa, wb, wvt)


def _rwkv_kernel(p_ref, mu_ref, w0_ref, a0_ref, wwa_ref, g2_ref, kk_ref, ka_ref, rk_ref,
                 lg_ref, lb_ref, seg_ref, tri_ref, blk_ref, o_ref,
                 s_ref, last_ref, ar_ref, bk_ref, bkh_ref, zv_ref, gc_ref, ry_ref, mn_ref, y_ref,
                 bonus_ref, gate_ref, *, width, heads):
    t_idx = pl.program_id(1)
    tt = p_ref.shape[0]
    nchunk = tt // CHUNK
    hd = RWKV_HEAD
    hs = range(heads)

    @pl.when(t_idx == 0)
    def _():
        s_ref[...] = jnp.zeros(s_ref.shape, F32)
        last_ref[...] = jnp.zeros(last_ref.shape, F32)
        zv_ref[...] = jnp.zeros(zv_ref.shape, BF16)

    p = p_ref[...]
    prev = pltpu.roll(p, shift=1, axis=0)
    row0 = lax.broadcasted_iota(jnp.int32, (tt, 1), 0) == 0
    prev = jnp.where(row0, last_ref[0:1, :], prev)
    last_ref[0:1, :] = p[tt - 1:tt, :]
    pm = p + (prev - p) * mu_ref[...]

    r = pm[:, 0:width]
    k = pm[:, width:2 * width]
    v = pm[:, 2 * width:3 * width]
    wa = pm[:, 3 * width:3 * width + DECAY_LORA + AAA_LORA]
    gd = pm[:, 3 * width + DECAY_LORA + AAA_LORA:]

    lane = lax.broadcasted_iota(jnp.int32, wa.shape, 1)
    z = jnp.where(lane < DECAY_LORA, jnp.tanh(wa), wa).astype(BF16)
    lora = _dot(z, wwa_ref[...])
    logw = -math.exp(-0.5) * jax.nn.sigmoid(w0_ref[...] + lora[:, :width])
    lr = jax.nn.sigmoid(a0_ref[...] + lora[:, width:])
    gate_ref[...] = _dot(jax.nn.sigmoid(gd).astype(BF16), g2_ref[...])

    seg = seg_ref[...]

    def segsum(x):
        hi, lo = _split2(x)
        return _dot(hi, seg) + _dot(lo, seg)

    kk = k * kk_ref[...]
    kk = kk / jnp.maximum(jnp.sqrt(segsum(kk * kk)), 1e-12)
    k2 = k * (1.0 + (lr - 1.0) * ka_ref[...])
    bonus_ref[...] = segsum(r * k2 * rk_ref[...]) * v

    h3 = _split3(logw)
    tri = tri_ref[...]
    blk = blk_ref[...]
    cum = _dot(tri, h3[0]) + _dot(tri, h3[1]) + _dot(tri, h3[2])
    tot = _dot(blk, h3[0]) + _dot(blk, h3[1]) + _dot(blk, h3[2])
    einv = jnp.exp(-cum)
    ed = jnp.exp(tot - cum)
    b = kk * lr
    parts = (
        (-kk * jnp.exp(cum - logw), ar_ref, 0), (r * jnp.exp(cum), ar_ref, CHUNK),
        (b * einv, bk_ref, 0), (k2 * einv, bk_ref, CHUNK),
        (b * ed, bkh_ref, 0), (k2 * ed, bkh_ref, CHUNK),
    )
    gc = jnp.exp(tot)
    zlane = jnp.zeros((tt, hd), F32)
    for h in hs:
        sl = slice(h * hd, (h + 1) * hd)
        gc_ref[h] = gc[:, sl]
        zv = jnp.concatenate([zlane, v[:, sl]], axis=1).astype(BF16)
        for c in range(nchunk):
            zv_ref[h, c, CHUNK:, :] = zv[c * CHUNK:(c + 1) * CHUNK]
        for val, ref, off in parts:
            vb = val[:, sl].astype(BF16)
            for c in range(nchunk):
                ref[h, c, off:off + CHUNK, :] = vb[c * CHUNK:(c + 1) * CHUNK]

    ri = lax.broadcasted_iota(jnp.int32, (2 * CHUNK, 2 * CHUNK), 0)
    ci = lax.broadcasted_iota(jnp.int32, (2 * CHUNK, 2 * CHUNK), 1)
    mask = ci % CHUNK < ri % CHUNK + ri // CHUNK
    e_r = lax.broadcasted_iota(jnp.int32, (CHUNK, CHUNK), 0)
    e_c = lax.broadcasted_iota(jnp.int32, (CHUNK, CHUNK), 1)
    eye = (e_r == e_c).astype(F32)
    zpad = jnp.zeros((CHUNK, hd), F32)

    def pre_body(c, carry):
        rows = pl.ds(pl.multiple_of(c * CHUNK, CHUNK), CHUNK)
        ar = [ar_ref[h, c] for h in hs]
        g = [jnp.where(mask, _dot_nt(ar[h], bk_ref[h, c]), 0.0) for h in hs]
        gb = [x.astype(BF16) for x in g]
        gv = [_dot(gb[h], zv_ref[h, c]) for h in hs]
        a = [x[:CHUNK, :hd] for x in g]
        ab = [x.astype(BF16) for x in a]
        pw = [_dot(x, x) for x in ab]
        xs = [eye + x for x in a]
        for _ in range(4):
            res = [_dot(jnp.concatenate([xs[h], pw[h]], axis=0).astype(BF16), pw[h].astype(BF16))
                   for h in hs]
            xs = [xs[h] + res[h][:CHUNK] for h in hs]
            pw = [res[h][CHUNK:] for h in hs]
        tinv = [xs[h] + _dot(xs[h].astype(BF16), pw[h].astype(BF16)) for h in hs]
        e = [jnp.concatenate([ar[h][:CHUNK].astype(F32), zpad], axis=1) + gv[h][:CHUNK] for h in hs]
        wb = [_dot(tinv[h].astype(BF16), e[h].astype(BF16)).astype(BF16) for h in hs]
        for h in hs:
            ry_ref[h, rows, :] = (jnp.concatenate([ar[h][CHUNK:].astype(F32), zpad], axis=1)
                                  + gv[h][CHUNK:] + _dot(gb[h][CHUNK:, :hd], wb[h]))
            left = jnp.concatenate([wb[h], zv_ref[h, c, CHUNK:, :]], axis=0)
            mn_ref[h, c] = _dot_tn(left, bkh_ref[h, c])
        return carry

    lax.fori_loop(0, nchunk, pre_body, 0)

    def seq_body(c, carry):
        r0 = pl.multiple_of(c * CHUNK, CHUNK)
        rows = pl.ds(r0, CHUNK)
        s0 = [s_ref[h] for h in hs]
        s0b = [x.astype(BF16) for x in s0]
        ry = [ry_ref[h, rows, :] for h in hs]
        mn = [mn_ref[h, c] for h in hs]
        for h in hs:
            y_ref[h, rows, :] = ry[h] + _dot_nt(ry[h][:, :hd].astype(BF16), s0b[h])
            s_ref[h, CHUNK:, :] = (s0[h][CHUNK:] * gc_ref[h, pl.ds(r0, 1), :]
                                   + _dot(s0b[h][CHUNK:], mn[h][:CHUNK].astype(BF16)) + mn[h][CHUNK:])
        return carry

    lax.fori_loop(0, nchunk, seq_body, 0)

    y = jnp.concatenate([y_ref[h][:, hd:] for h in hs], axis=-1)
    mean = segsum(y) * (1.0 / hd)
    dlt = y - mean
    var = segsum(dlt * dlt) * (1.0 / hd)
    yn = dlt * lax.rsqrt(var + GN_EPS) * lg_ref[...] + lb_ref[...]
    o_ref[...] = ((yn + bonus_ref[...]) * gate_ref[...]).astype(o_ref.dtype)


def _rwkv(p, batch, seq, mu, w0, w2, a0, a2, g2, k_k, k_a, r_k, lnx_g, lnx_b):
    n, cols = p.shape
    width = w0.shape[0]
    heads = width // RWKV_HEAD
    assert RWKV_HEAD == CHUNK
    tt = TT_RWKV
    nt = seq // tt
    nchunk = tt // CHUNK
    wwa = jnp.zeros((DECAY_LORA + AAA_LORA, 2 * width), F32)
    wwa = wwa.at[:DECAY_LORA, :width].set(w2).at[DECAY_LORA:, width:].set(a2).astype(BF16)
    lane_head = jnp.arange(width) // RWKV_HEAD
    seg = (lane_head[:, None] == lane_head[None, :]).astype(BF16)
    tch = jnp.arange(tt) // CHUNK
    same = tch[:, None] == tch[None, :]
    tri = (same & (jnp.arange(tt)[None, :] <= jnp.arange(tt)[:, None])).astype(BF16)
    blk = same.astype(BF16)
    row = lambda a: a.reshape(1, -1)
    vec = lambda w: pl.BlockSpec((1, w), lambda b, t: (0, 0))
    full2 = lambda a: pl.BlockSpec(a.shape, lambda b, t: (0, 0))
    slab = lambda lanes, dt: pltpu.VMEM((heads, nchunk, 2 * CHUNK, lanes), dt)
    kern = functools.partial(_rwkv_kernel, width=width, heads=heads)
    return pl.pallas_call(
        kern,
        grid=(batch, nt),
        in_specs=[
            pl.BlockSpec((tt, cols), lambda b, t: (b * nt + t, 0)),
            vec(cols), vec(width), vec(width), full2(wwa), pl.BlockSpec(g2.shape, lambda b, t: (0, 0)),
            vec(width), vec(width), vec(width), vec(width), vec(width),
            full2(seg), full2(tri), full2(blk),
        ],
        out_specs=pl.BlockSpec((tt, width), lambda b, t: (b * nt + t, 0)),
        out_shape=jax.ShapeDtypeStruct((n, width), BF16),
        scratch_shapes=[
            pltpu.VMEM((heads, 2 * RWKV_HEAD, RWKV_HEAD), F32),
            pltpu.VMEM((8, cols), F32),
            slab(RWKV_HEAD, BF16),
            slab(RWKV_HEAD, BF16),
            slab(RWKV_HEAD, BF16),
            slab(2 * RWKV_HEAD, BF16),
            pltpu.VMEM((heads, tt, RWKV_HEAD), F32),
            pltpu.VMEM((heads, tt, 2 * RWKV_HEAD), F32),
            slab(RWKV_HEAD, F32),
            pltpu.VMEM((heads, tt, 2 * RWKV_HEAD), F32),
            pltpu.VMEM((tt, width), F32), pltpu.VMEM((tt, width), F32),
        ],
        compiler_params=_cparams(("arbitrary", "arbitrary")),
        name="rwkv",
    )(p, row(mu), row(w0), row(a0), wwa, g2.astype(BF16), row(k_k), row(k_a), row(r_k),
      row(lnx_g), row(lnx_b), seg, tri, blk)


def _diff_kernel(lq1_ref, lk1_ref, lq2_ref, lk2_ref, sg_ref, q_ref, k_ref, vt_ref, o_ref,
                 m0_ref, a0_ref, m1_ref, a1_ref, sa0_ref, sa1_ref, sb0_ref, sb1_ref):
    i = pl.program_id(2)
    tq = q_ref.shape[0]
    tk = TK_ATT
    d = DIFF_HEAD
    hw = 2 * d
    nchunk = tk // CHUNK
    neg = -1e30
    qf = q_ref[...].astype(F32) * (d ** -0.5 * math.log2(math.e))
    lane = lax.broadcasted_iota(jnp.int32, qf.shape, 1)
    rchunk = lax.broadcasted_iota(jnp.int32, qf.shape, 0) // CHUNK
    q0 = jnp.where(lane < d, qf, 0.0).astype(BF16)
    q1 = jnp.where(lane >= d, qf, 0.0).astype(BF16)
    c0 = lane - d
    c1 = lane
    in0 = c0 * (nchunk - 1 - c0) >= 0
    in1 = c1 < nchunk
    qm0 = jnp.where(in0, jnp.where(rchunk < c0, neg, 0.0), 0.0).astype(BF16)
    qm1 = jnp.where(in1, jnp.where(rchunk < c1, neg, 0.0), 0.0).astype(BF16)
    keep0 = jnp.where(in0, 0.0, 1.0).astype(BF16)
    keep1 = jnp.where(in1, 0.0, 1.0).astype(BF16)
    hot0 = jnp.where(in0, jnp.where(rchunk == c0, 1.0, 0.0), 0.0).astype(BF16)
    hot1 = jnp.where(in1, jnp.where(rchunk == c1, 1.0, 0.0), 0.0).astype(BF16)

    for m_ref, a_ref in ((m0_ref, a0_ref), (m1_ref, a1_ref)):
        m_ref[...] = jnp.full(m_ref.shape, neg, F32)
        a_ref[...] = jnp.zeros(a_ref.shape, F32)

    def produce(s0_ref, s1_ref, j):
        diag = jnp.where(j == i, 1.0, 0.0).astype(BF16)
        kb = k_ref[pl.ds(pl.multiple_of(j * tk, tk), tk), :]
        s0_ref[...] = _dot_nt(kb * keep0 + hot0, q0 + qm0 * diag)
        s1_ref[...] = _dot_nt(kb * keep1 + hot1, q1 + qm1 * diag)

    def update(st, vt, m_ref, a_ref):
        m_old = m_ref[...]
        m_new = jnp.maximum(m_old, jnp.max(st, axis=0, keepdims=True))
        alpha = jnp.exp2(m_old - m_new)
        pt = jnp.exp2(st - m_new).astype(BF16)
        a_ref[...] = alpha * a_ref[...] + _dot(vt, pt)
        m_ref[...] = m_new

    def consume(s0_ref, s1_ref, j):
        vt = vt_ref[:, pl.ds(pl.multiple_of(j * tk, tk), tk)]
        update(s0_ref[...], vt, m0_ref, a0_ref)
        update(s1_ref[...], vt, m1_ref, a1_ref)

    nb = i + 1
    npairs = (nb - 1) // 2
    produce(sa0_ref, sa1_ref, 0)

    def body(t, carry):
        j = 2 * t
        produce(sb0_ref, sb1_ref, j + 1)
        consume(sa0_ref, sa1_ref, j)
        produce(sa0_ref, sa1_ref, j + 2)
        consume(sb0_ref, sb1_ref, j + 1)
        return carry

    lax.fori_loop(0, npairs, body, 0)
    last = 2 * npairs

    @pl.when(last == i)
    def _():
        consume(sa0_ref, sa1_ref, last)

    @pl.when(last < i)
    def _():
        produce(sb0_ref, sb1_ref, last + 1)
        consume(sa0_ref, sa1_ref, last)
        consume(sb0_ref, sb1_ref, last + 1)

    lam = (jnp.exp(jnp.sum(lq1_ref[...] * lk1_ref[...], axis=-1, keepdims=True))
           - jnp.exp(jnp.sum(lq2_ref[...] * lk2_ref[...], axis=-1, keepdims=True)) + LAMBDA_INIT)
    ot = (a0_ref[:hw, :] / a0_ref[hw:hw + 1, :] - lam * (a1_ref[:hw, :] / a1_ref[hw:hw + 1, :]))
    o = _rms(ot.T, sg_ref[...]) * (1.0 - LAMBDA_INIT)
    o_ref[...] = o.astype(o_ref.dtype)


def _diffattn(qk, vt, batch, seq, lq1, lk1, lq2, lk2, subln_g):
    n = qk.shape[0]
    width = qk.shape[1] // 2
    hw = 2 * DIFF_HEAD
    heads = width // hw
    tq = TQ_ATT
    assert TK_ATT == tq and seq % tq == 0
    nq = seq // tq
    vec = lambda w: pl.BlockSpec((1, w), lambda b, h, i: (0, 0))
    row = lambda a: a.reshape(1, -1)
    stat = pltpu.VMEM((1, tq), F32)
    acc = pltpu.VMEM((VT_ROWS, tq), F32)
    return pl.pallas_call(
        _diff_kernel,
        grid=(batch, heads, nq),
        in_specs=[
            vec(DIFF_HEAD), vec(DIFF_HEAD), vec(DIFF_HEAD), vec(DIFF_HEAD), vec(hw),
            pl.BlockSpec((tq, hw), lambda b, h, i: (b * nq + i, h)),
            pl.BlockSpec((seq, hw), lambda b, h, i: (b, heads + h)),
            pl.BlockSpec((VT_ROWS, seq), lambda b, h, i: (h, b)),
        ],
        out_specs=pl.BlockSpec((tq, hw), lambda b, h, i: (b * nq + i, h)),
        out_shape=jax.ShapeDtypeStruct((n, width), BF16),
        scratch_shapes=[stat, acc, stat, acc] + [pltpu.VMEM((TK_ATT, tq), F32)] * 4,
        compiler_params=_cparams(("arbitrary", "arbitrary", "arbitrary")),
        name="diffattn",
    )(row(lq1), row(lk1), row(lq2), row(lk2), row(subln_g), qk, qk, vt)


def _memkv_kernel(m_ref, g_ref, w_ref, o_ref):
    o_ref[...] = _dot(_rms(m_ref[...], g_ref[...]).astype(BF16), w_ref[...]).astype(o_ref.dtype)


def _memkv(mem2, g, wkv):
    n, d = mem2.shape
    c = wkv.shape[1]
    tn = 512
    return pl.pallas_call(
        _memkv_kernel,
        grid=(c // tn,),
        in_specs=[
            pl.BlockSpec((n, d), lambda j: (0, 0)),
            pl.BlockSpec((1, d), lambda j: (0, 0)),
            pl.BlockSpec((d, tn), lambda j: (0, j)),
        ],
        out_specs=pl.BlockSpec((n, tn), lambda j: (0, j)),
        out_shape=jax.ShapeDtypeStruct((n, c), BF16),
        compiler_params=_cparams(("arbitrary",)),
        name="memkv",
    )(mem2, g.reshape(1, d), wkv.astype(BF16))


def _mid_kernel(x_ref, yr_ref, yd_ref, wo1_ref, wo2_ref, g2_ref, wq_ref, km_ref, vm_ref, wox_ref,
                g3_ref, rw_ref, rb_ref, tril_ref,
                h_ref, xn_ref, ti_ref, gt_ref, rk_ref, cnt_ref, base_ref):
    step = pl.program_id(0) * pl.num_programs(1) + pl.program_id(1)
    tm, dm = x_ref.shape
    hd = dm // XATTN_HEADS

    @pl.when(step == 0)
    def _():
        base_ref[...] = jnp.zeros(base_ref.shape, F32)

    h1 = x_ref[...] + _dot(yr_ref[...], wo1_ref[...]) + _dot(yd_ref[...], wo2_ref[...])
    hn = _rms(h1, g2_ref[...]).astype(BF16)
    q = (_dot(hn, wq_ref[...]) * (hd ** -0.5)).astype(BF16)
    outs = []
    for hh in range(XATTN_HEADS):
        sl = slice(hh * hd, (hh + 1) * hd)
        s = _dot_nt(q[:, sl], km_ref[:, sl])
        s = s - jnp.max(s, axis=-1, keepdims=True)
        e = jnp.exp(s)
        pr = e / jnp.sum(e, axis=-1, keepdims=True)
        outs.append(_dot(pr.astype(BF16), vm_ref[:, sl]))
    o = jnp.concatenate(outs, axis=-1).astype(BF16)
    h2 = h1 + _dot(o, wox_ref[...])
    h_ref[...] = h2
    xn = _rms(h2, g3_ref[...])
    _store_rows(xn_ref, xn)

    x_hi, x_lo = _split2(xn)
    logits = _dot(x_hi, rw_ref[0]) + (_dot(x_hi, rw_ref[1]) + _dot(x_lo, rw_ref[0])) + rb_ref[...]

    ne = logits.shape[1]
    eio = lax.broadcasted_iota(jnp.int32, (tm, ne), 1).astype(F32)
    work = logits
    vals, idxs, sels = [], [], []
    for _ in range(TOP_K):
        mx = jnp.max(work, axis=-1, keepdims=True)
        ix = jnp.min(jnp.where(work == mx, eio, float(ne)), axis=-1, keepdims=True)
        sel = eio == ix
        vals.append(mx)
        idxs.append(ix)
        sels.append(sel)
        work = jnp.where(sel, -jnp.inf, work)
    es = [jnp.exp(vv - vals[0]) for vv in vals]
    den = es[0] + es[1] + es[2] + es[3]
    gates = [ee / den for ee in es]

    assign = jnp.zeros((tm, ne), F32)
    for sel in sels:
        assign = assign + sel.astype(F32)
    before = _dot(tril_ref[...], assign.astype(BF16)) + base_ref[...]
    ranks = [jnp.sum(jnp.where(sel, before, 0.0), axis=-1, keepdims=True) for sel in sels]
    base_new = base_ref[...] + jnp.sum(assign, axis=0, keepdims=True)
    base_ref[...] = base_new
    cnt_ref[...] = base_new.astype(jnp.int32)

    kio = lax.broadcasted_iota(jnp.int32, (tm, TOP_K), 1)

    def cols(parts):
        out = jnp.where(kio == 0, parts[0], parts[1])
        for kk in range(2, TOP_K):
            out = jnp.where(kio == kk, parts[kk], out)
        return out

    ti_ref[...] = cols(idxs).astype(jnp.int32)
    gt_ref[...] = cols(gates)
    rk_ref[...] = cols(ranks).astype(jnp.int32)


def _mid(x2, yr, yd, batch, seq, w_out, norm2_g, wq, kv, wo, norm3_g, router_w, router_b):
    n, d = x2.shape
    tm = TM_MID
    nt = seq // tm
    mem_len = kv.shape[0] // batch
    wr = yr.shape[1]
    wo1 = w_out[:wr].astype(BF16)
    wo2 = w_out[wr:].astype(BF16)
    rw = jnp.stack(_split2(router_w))
    ne = router_w.shape[1]
    tril = (jnp.arange(tm)[None, :] < jnp.arange(tm)[:, None]).astype(BF16)
    tok = lambda w: pl.BlockSpec((tm, w), lambda b, t: (b * nt + t, 0))
    cst = lambda a: pl.BlockSpec(a.shape, lambda b, t: (0,) * a.ndim)
    row = lambda a: a.reshape(1, -1)
    return pl.pallas_call(
        _mid_kernel,
        grid=(batch, nt),
        in_specs=[
            tok(d), tok(wr), tok(d - wr), cst(wo1), cst(wo2), cst(row(norm2_g)), cst(wq),
            pl.BlockSpec((mem_len, d), lambda b, t: (b, 0)),
            pl.BlockSpec((mem_len, d), lambda b, t: (b, 1)),
            cst(wo), cst(row(norm3_g)), cst(rw), cst(row(router_b)), cst(tril),
        ],
        out_specs=[tok(d), pl.BlockSpec((tm * ROW_TILE, LANES), lambda b, t: (b * nt + t, 0)),
                   tok(TOP_K), tok(TOP_K), tok(TOP_K),
                   pl.BlockSpec((1, ne), lambda b, t: (0, 0))],
        out_shape=[
            jax.ShapeDtypeStruct((n, d), F32),
            jax.ShapeDtypeStruct((n * ROW_TILE, LANES), F32),
            jax.ShapeDtypeStruct((n, TOP_K), jnp.int32),
            jax.ShapeDtypeStruct((n, TOP_K), F32),
            jax.ShapeDtypeStruct((n, TOP_K), jnp.int32),
            jax.ShapeDtypeStruct((1, ne), jnp.int32),
        ],
        scratch_shapes=[pltpu.VMEM((1, ne), F32)],
        compiler_params=_cparams(("arbitrary", "arbitrary")),
        name="mid",
    )(x2, yr, yd, wo1, wo2, row(norm2_g), wq.astype(BF16), kv, kv, wo.astype(BF16),
      row(norm3_g), rw, row(router_b), tril)


def _dispatch_kernel(dest_ref, pad_start_ref, pad_len_ref, x_ref, o_ref, zero_ref, sem, zsem):
    i = pl.program_id(0)
    tm = x_ref.shape[0] // ROW_TILE

    @pl.when(i == 0)
    def _():
        zero_ref[...] = jnp.zeros(zero_ref.shape, zero_ref.dtype)
        pieces = [MOE_TILE >> (s + 1) for s in range(MOE_TILE.bit_length() - 1)]

        def zcopy(nrows, dst_row):
            return pltpu.make_async_copy(
                zero_ref.at[pl.ds(0, nrows * ROW_TILE), :],
                o_ref.at[pl.ds(pl.multiple_of(dst_row * ROW_TILE, ROW_TILE), nrows * ROW_TILE), :], zsem)

        def per_expert(e, carry):
            row = pad_start_ref[e]
            left = pad_len_ref[e]
            for piece in pieces:
                hit = (left & piece) != 0

                @pl.when(hit)
                def _():
                    zcopy(piece, row).start()

                row = row + jnp.where(hit, piece, 0)
            row = pad_start_ref[e]
            for piece in pieces:
                hit = (left & piece) != 0

                @pl.when(hit)
                def _():
                    zcopy(piece, row).wait()

                row = row + jnp.where(hit, piece, 0)
            return carry

        lax.fori_loop(0, N_EXPERTS, per_expert, 0)

        half = MOE_TILE // 2
        tail_start = pad_start_ref[N_EXPERTS]
        tail_pieces = pad_len_ref[N_EXPERTS] // half

        def tail_start_copy(t, carry):
            zcopy(half, tail_start + t * half).start()
            return carry

        def tail_wait_copy(t, carry):
            zcopy(half, tail_start).wait()
            return carry

        lax.fori_loop(0, tail_pieces, tail_start_copy, 0)
        lax.fori_loop(0, tail_pieces, tail_wait_copy, 0)

    def copy(r8, dst8):
        return pltpu.make_async_copy(x_ref.at[pl.ds(pl.multiple_of(r8, ROW_TILE), ROW_TILE), :],
                                     o_ref.at[pl.ds(pl.multiple_of(dst8, ROW_TILE), ROW_TILE), :], sem)

    def start(r, carry):
        for kk in range(TOP_K):
            copy(r * ROW_TILE, dest_ref[(i * tm + r) * TOP_K + kk]).start(priority=kk % 2)
        return carry

    lax.fori_loop(0, tm, start, 0, unroll=DMA_UNROLL)

    def wait(r, carry):
        for kk in range(TOP_K):
            copy(0, 0).wait()
        return carry

    lax.fori_loop(0, tm, wait, 0, unroll=WAIT_UNROLL)


def _dispatch(dest8, pad_start, pad_len, xn, rows_padded):
    tm = TM_DISP
    n = xn.shape[0] // ROW_TILE
    return pl.pallas_call(
        _dispatch_kernel,
        grid_spec=pltpu.PrefetchScalarGridSpec(
            num_scalar_prefetch=3,
            grid=(n // tm,),
            in_specs=[pl.BlockSpec((tm * ROW_TILE, LANES), lambda i, *_: (i, 0))],
            out_specs=pl.BlockSpec(memory_space=pl.ANY),
            scratch_shapes=[pltpu.VMEM((MOE_TILE // 2 * ROW_TILE, LANES), xn.dtype),
                            pltpu.SemaphoreType.DMA(()), pltpu.SemaphoreType.DMA(())],
        ),
        out_shape=jax.ShapeDtypeStruct((rows_padded * ROW_TILE, LANES), xn.dtype),
        compiler_params=_cparams(("arbitrary",)),
        name="dispatch",
    )(dest8, pad_start, pad_len, xn)


def _experts_kernel(be_ref, nu_ref, x_ref, w1_ref, b1_ref, w2_ref, b2_ref, o_ref,
                    w1b_ref, w2b_ref, prev_ref):
    j = pl.program_id(0)
    e = be_ref[j]
    ff = w2_ref.shape[0]

    @pl.when(j == 0)
    def _():
        prev_ref[0] = -1

    @pl.when(j < nu_ref[0])
    def _():
        @pl.when(e != prev_ref[0])
        def _():
            w1b_ref[...] = w1_ref[...].astype(BF16)
            w2b_ref[...] = w2_ref[...].astype(BF16)
            prev_ref[0] = e

        h = _dot(_load_rows(x_ref, MOE_TILE).astype(BF16), w1b_ref[...]) + b1_ref[...]
        gate = jnp.minimum(h[:, :ff], SWIGLU_LIMIT)
        lin = jnp.clip(h[:, ff:], -SWIGLU_LIMIT, SWIGLU_LIMIT)
        act = gate * jax.nn.sigmoid(SWIGLU_ALPHA * gate) * (lin + 1.0)
        _store_rows(o_ref, _dot(act.astype(BF16), w2b_ref[...]) + b2_ref[...])

    @pl.when(j >= nu_ref[0])
    def _():
        o_ref[...] = jnp.zeros(o_ref.shape, F32)


def _experts(block_e, nused, xin, w1, b1, w2, b2):
    ne, d, f2 = w1.shape
    rows = xin.shape[0] // ROW_TILE
    ff = w2.shape[1]
    nblk = rows // MOE_TILE
    return pl.pallas_call(
        _experts_kernel,
        grid_spec=pltpu.PrefetchScalarGridSpec(
            num_scalar_prefetch=2,
            grid=(nblk,),
            in_specs=[
                pl.BlockSpec((MOE_TILE * ROW_TILE, LANES), lambda j, be, nu: (jnp.minimum(j, nu[0] - 1), 0)),
                pl.BlockSpec((None, d, f2), lambda j, be, nu: (be[j], 0, 0)),
                pl.BlockSpec((None, 1, f2), lambda j, be, nu: (be[j], 0, 0)),
                pl.BlockSpec((None, ff, d), lambda j, be, nu: (be[j], 0, 0)),
                pl.BlockSpec((None, 1, d), lambda j, be, nu: (be[j], 0, 0)),
            ],
            out_specs=pl.BlockSpec((MOE_TILE * ROW_TILE, LANES), lambda j, be, nu: (j, 0)),
            scratch_shapes=[pltpu.VMEM((d, f2), BF16), pltpu.VMEM((ff, d), BF16),
                            pltpu.SMEM((1,), jnp.int32)],
        ),
        out_shape=jax.ShapeDtypeStruct(xin.shape, F32),
        compiler_params=_cparams(("arbitrary",)),
        name="experts",
    )(block_e, nused, xin, w1, b1.reshape(ne, 1, f2), w2, b2.reshape(ne, 1, d))


def _combine_kernel(dest_ref, h_ref, gt_ref, g_ref, yb_ref, o_ref, buf_ref, sem):
    i = pl.program_id(0)
    tm = h_ref.shape[0]
    slot = i % 2

    def copy(src8, dst_slot, kk, r8):
        return pltpu.make_async_copy(
            yb_ref.at[pl.ds(pl.multiple_of(src8, ROW_TILE), ROW_TILE), :],
            buf_ref.at[dst_slot, kk, pl.ds(pl.multiple_of(r8, ROW_TILE), ROW_TILE), :], sem.at[dst_slot])

    def issue(tile, dst_slot):
        def start(r, carry):
            for kk in range(TOP_K):
                copy(dest_ref[(tile * tm + r) * TOP_K + kk], dst_slot, kk, r * ROW_TILE).start(priority=kk % 2)
            return carry

        lax.fori_loop(0, tm, start, 0, unroll=DMA_UNROLL)

    @pl.when(i == 0)
    def _():
        issue(0, 0)

    @pl.when(i + 1 < pl.num_programs(0))
    def _():
        issue(i + 1, 1 - slot)

    def wait(r, carry):
        for kk in range(TOP_K):
            copy(0, slot, kk, 0).wait()
        return carry

    lax.fori_loop(0, tm, wait, 0, unroll=WAIT_UNROLL)

    gt = gt_ref[...]
    rg = COMB_ROWS
    for g0 in range(0, tm, rg):
        rows = slice(g0, g0 + rg)
        parts = []
        for j in range(ROW_TILE):
            y = h_ref[rows, j * LANES:(j + 1) * LANES]
            for kk in range(TOP_K):
                y = y + gt[rows, kk:kk + 1] * buf_ref[slot, kk, pl.ds(g0 * ROW_TILE + j, rg, stride=ROW_TILE), :]
            parts.append(y)
        y = jnp.concatenate(parts, axis=1)
        o_ref[rows, :] = _rms(y, g_ref[...])


def _combine(dest_flat, h2, gates, final_g, yb):
    n, d = h2.shape
    tm = TM_COMB
    return pl.pallas_call(
        _combine_kernel,
        grid_spec=pltpu.PrefetchScalarGridSpec(
            num_scalar_prefetch=1,
            grid=(n // tm,),
            in_specs=[
                pl.BlockSpec((tm, d), lambda i, dest: (i, 0)),
                pl.BlockSpec((tm, TOP_K), lambda i, dest: (i, 0)),
                pl.BlockSpec((1, d), lambda i, dest: (0, 0)),
                pl.BlockSpec(memory_space=pl.ANY),
            ],
            out_specs=pl.BlockSpec((tm, d), lambda i, dest: (i, 0)),
            scratch_shapes=[pltpu.VMEM((2, TOP_K, tm * ROW_TILE, LANES), F32), pltpu.SemaphoreType.DMA((2,))],
        ),
        out_shape=jax.ShapeDtypeStruct((n, d), F32),
        compiler_params=_cparams(("arbitrary",)),
        name="combine",
    )(dest_flat, h2, gates, final_g.reshape(1, d), yb)


def kernel(x, mem, norm1_g, w_in, mu_shift, w0, w2, a0, a2, g2, k_k, k_a, r_k, lnx_g, lnx_b,
           lam_q1, lam_k1, lam_q2, lam_k2, subln_g, w_out, norm2_g, mem_norm_g, wq_x, wkv_x, wo_x,
           norm3_g, router_w, router_b, moe_w1, moe_b1, moe_w2, moe_b2, final_g):
    batch, seq, d = x.shape
    n = batch * seq
    shift_cols = mu_shift.shape[1]
    x2 = x.reshape(n, d)

    p, qk, vt = _inproj(x2, norm1_g[0], w_in[0], shift_cols)
    y_rwkv = _rwkv(p, batch, seq, mu_shift[0], w0[0], w2[0], a0[0], a2[0], g2[0], k_k[0], k_a[0],
                   r_k[0].reshape(-1), lnx_g[0], lnx_b[0])
    y_diff = _diffattn(qk, vt, batch, seq, lam_q1[0], lam_k1[0], lam_q2[0], lam_k2[0], subln_g[0])
    kv = _memkv(mem.reshape(-1, d), mem_norm_g[0], wkv_x[0])

    h2, xn3, top_i, gates, rank, counts = _mid(
        x2, y_rwkv, y_diff, batch, seq, w_out[0], norm2_g[0], wq_x[0], kv, wo_x[0], norm3_g[0],
        router_w[0], router_b[0])

    counts = counts.reshape(-1)
    padded = ((counts + MOE_TILE - 1) // MOE_TILE) * MOE_TILE
    padded_end = jnp.cumsum(padded)
    padded_off = padded_end - padded
    rows_padded = n * TOP_K + N_EXPERTS * MOE_TILE
    nblk = rows_padded // MOE_TILE
    block_start = jnp.arange(nblk, dtype=jnp.int32) * MOE_TILE
    block_e = jnp.minimum(jnp.sum(block_start[:, None] >= padded_end[None, :], axis=1),
                          N_EXPERTS - 1).astype(jnp.int32)
    nused = (padded_end[-1:] // MOE_TILE).astype(jnp.int32)
    expert_hot = top_i[:, :, None] == jnp.arange(N_EXPERTS, dtype=jnp.int32)
    row_off = jnp.sum(jnp.where(expert_hot, padded_off.astype(jnp.int32), 0), axis=-1)
    dest8 = ((row_off + rank) * ROW_TILE).astype(jnp.int32).reshape(-1)
    pad_start = jnp.concatenate([padded_off + counts, padded_end[-1:]]).astype(jnp.int32)
    pad_len = jnp.concatenate([padded - counts, rows_padded - padded_end[-1:]]).astype(jnp.int32)

    xin = _dispatch(dest8, pad_start, pad_len, xn3, rows_padded)
    yb = _experts(block_e, nused, xin, moe_w1[0], moe_b1[0], moe_w2[0], moe_b2[0])
    out = _combine(dest8, h2, gates, final_g, yb)
    return out.reshape(batch, seq, d)
```

```python
import functools
import math

import jax
import jax.numpy as jnp
from jax import lax
from jax.experimental import pallas as pl
from jax.experimental.pallas import tpu as pltpu

F32 = jnp.float32
BF16 = jnp.bfloat16

CHUNK = 64
RWKV_HEAD = 64
DECAY_LORA = 64
AAA_LORA = 64
GATE_LORA = 128
DIFF_HEAD = 64
XATTN_HEADS = 4
N_EXPERTS = 32
TOP_K = 4
SWIGLU_LIMIT = 7.0
SWIGLU_ALPHA = 1.702
NORM_EPS = 1e-5
GN_EPS = 64e-5
LAMBDA_INIT = 0.8 - 0.6 * math.exp(-0.3 * 0)

VMEM_LIMIT = 56 * 1024 * 1024
TM_PROJ = 512
TT_RWKV = 256
PRE_CHUNKS = 4
VT_ROWS = 144
TQ_ATT = 512
TK_ATT = 512
TM_MID = 512
TM_DISP = 512
TM_COMB = 512
MOE_TILE = 512
DMA_UNROLL = 4
COMB_ROWS = 32
WAIT_UNROLL = 32
ROW_TILE = 8
LANES = 128


def _cparams(sem):
    return pltpu.CompilerParams(dimension_semantics=sem, vmem_limit_bytes=VMEM_LIMIT)


def _dot(a, b):
    return jnp.dot(a, b, preferred_element_type=F32)


def _dot_nt(a, b):
    return lax.dot_general(a, b, (((1,), (1,)), ((), ())), preferred_element_type=F32)


def _dot_tn(a, b):
    return lax.dot_general(a, b, (((0,), (0,)), ((), ())), preferred_element_type=F32)


def _rms(x, g):
    return x * lax.rsqrt(jnp.mean(x * x, axis=-1, keepdims=True) + NORM_EPS) * g


def _load_rows(ref, nrows):
    return jnp.concatenate([ref[pl.ds(j, nrows, stride=ROW_TILE), :] for j in range(ROW_TILE)], axis=1)


def _store_rows(ref, val):
    nrows = val.shape[0]
    for j in range(ROW_TILE):
        ref[pl.ds(j, nrows, stride=ROW_TILE), :] = val[:, j * LANES:(j + 1) * LANES]


def _split2(x):
    hi = x.astype(BF16)
    lo = (x - hi.astype(F32)).astype(BF16)
    return hi, lo


def _split3(x):
    hi = x.astype(BF16)
    r1 = x - hi.astype(F32)
    mid = r1.astype(BF16)
    lo = (r1 - mid.astype(F32)).astype(BF16)
    return hi, mid, lo


def _inproj_kernel(x_ref, g_ref, wa_ref, wb_ref, wvt_ref, p_ref, qk_ref, vt_ref):
    xn = _rms(x_ref[...], g_ref[...]).astype(BF16)
    p_ref[...] = _dot(xn, wa_ref[...])
    qk_ref[...] = _dot(xn, wb_ref[...]).astype(BF16)
    vt = _dot_nt(wvt_ref[...], xn)
    hw = 2 * DIFF_HEAD
    ones = jnp.ones((VT_ROWS - hw, vt.shape[1]), F32)
    for h in range(vt.shape[0] // hw):
        vt_ref[h * VT_ROWS:(h + 1) * VT_ROWS, :] = jnp.concatenate(
            [vt[h * hw:(h + 1) * hw], ones], axis=0).astype(BF16)


def _inproj(x2, g, w_in, shift_cols):
    n, d = x2.shape
    dw = (w_in.shape[1] - shift_cols) // 3
    wa = w_in[:, :shift_cols].astype(BF16)
    wb = w_in[:, shift_cols:shift_cols + 2 * dw].astype(BF16)
    wvt = w_in[:, shift_cols + 2 * dw:].T.astype(BF16)
    tm = TM_PROJ
    return pl.pallas_call(
        _inproj_kernel,
        grid=(n // tm,),
        in_specs=[
            pl.BlockSpec((tm, d), lambda i: (i, 0)),
            pl.BlockSpec((1, d), lambda i: (0, 0)),
            pl.BlockSpec((d, shift_cols), lambda i: (0, 0)),
            pl.BlockSpec((d, 2 * dw), lambda i: (0, 0)),
            pl.BlockSpec((dw, d), lambda i: (0, 0)),
        ],
        out_specs=[
            pl.BlockSpec((tm, shift_cols), lambda i: (i, 0)),
            pl.BlockSpec((tm, 2 * dw), lambda i: (i, 0)),
            pl.BlockSpec((dw // (2 * DIFF_HEAD) * VT_ROWS, tm), lambda i: (0, i)),
        ],
        out_shape=[
            jax.ShapeDtypeStruct((n, shift_cols), F32),
            jax.ShapeDtypeStruct((n, 2 * dw), BF16),
            jax.ShapeDtypeStruct((dw // (2 * DIFF_HEAD) * VT_ROWS, n), BF16),
        ],
        compiler_params=_cparams(("arbitrary",)),
        name="inproj",
    )(x2, g.reshape(1, d), wa, wb, wvt)


def _rwkv_kernel(p_ref, mu_ref, w0_ref, a0_ref, wwa_ref, g2_ref, kk_ref, ka_ref, rk_ref,
                 lg_ref, lb_ref, seg_ref, tri_ref, blk_ref, o_ref,
                 s_ref, last_ref, ar_ref, bk_ref, bkh_ref, zv_ref, gc_ref, ry_ref, mn_ref, y_ref,
                 bonus_ref, gate_ref, *, width, heads):
    t_idx = pl.program_id(1)
    tt = p_ref.shape[0]
    nchunk = tt // CHUNK
    hd = RWKV_HEAD
    hs = range(heads)

    @pl.when(t_idx == 0)
    def _():
        s_ref[...] = jnp.zeros(s_ref.shape, F32)
        last_ref[...] = jnp.zeros(last_ref.shape, F32)
        zv_ref[...] = jnp.zeros(zv_ref.shape, BF16)

    p = p_ref[...]
    prev = pltpu.roll(p, shift=1, axis=0)
    row0 = lax.broadcasted_iota(jnp.int32, (tt, 1), 0) == 0
    prev = jnp.where(row0, last_ref[0:1, :], prev)
    last_ref[0:1, :] = p[tt - 1:tt, :]
    pm = p + (prev - p) * mu_ref[...]

    r = pm[:, 0:width]
    k = pm[:, width:2 * width]
    v = pm[:, 2 * width:3 * width]
    wa = pm[:, 3 * width:3 * width + DECAY_LORA + AAA_LORA]
    gd = pm[:, 3 * width + DECAY_LORA + AAA_LORA:]

    lane = lax.broadcasted_iota(jnp.int32, wa.shape, 1)
    z = jnp.where(lane < DECAY_LORA, jnp.tanh(wa), wa).astype(BF16)
    lora = _dot(z, wwa_ref[...])
    logw = -math.exp(-0.5) * jax.nn.sigmoid(w0_ref[...] + lora[:, :width])
    lr = jax.nn.sigmoid(a0_ref[...] + lora[:, width:])
    gate_ref[...] = _dot(jax.nn.sigmoid(gd).astype(BF16), g2_ref[...])

    seg = seg_ref[...]

    def segsum(x):
        hi, lo = _split2(x)
        return _dot(hi, seg) + _dot(lo, seg)

    kk = k * kk_ref[...]
    kk = kk / jnp.maximum(jnp.sqrt(segsum(kk * kk)), 1e-12)
    k2 = k * (1.0 + (lr - 1.0) * ka_ref[...])
    bonus_ref[...] = segsum(r * k2 * rk_ref[...]) * v

    h3 = _split3(logw)
    tri = tri_ref[...]
    blk = blk_ref[...]
    cum = _dot(tri, h3[0]) + _dot(tri, h3[1]) + _dot(tri, h3[2])
    tot = _dot(blk, h3[0]) + _dot(blk, h3[1]) + _dot(blk, h3[2])
    einv = jnp.exp(-cum)
    ed = jnp.exp(tot - cum)
    b = kk * lr
    parts = (
        (-kk * jnp.exp(cum - logw), ar_ref, 0), (r * jnp.exp(cum), ar_ref, CHUNK),
        (b * einv, bk_ref, 0), (k2 * einv, bk_ref, CHUNK),
        (b * ed, bkh_ref, 0), (k2 * ed, bkh_ref, CHUNK),
    )
    gc = jnp.exp(tot)
    zlane = jnp.zeros((tt, hd), F32)
    for h in hs:
        sl = slice(h * hd, (h + 1) * hd)
        gc_ref[h] = gc[:, sl]
        zv = jnp.concatenate([zlane, v[:, sl]], axis=1).astype(BF16)
        for c in range(nchunk):
            zv_ref[h, c, CHUNK:, :] = zv[c * CHUNK:(c + 1) * CHUNK]
        for val, ref, off in parts:
            vb = val[:, sl].astype(BF16)
            for c in range(nchunk):
                ref[h, c, off:off + CHUNK, :] = vb[c * CHUNK:(c + 1) * CHUNK]

    ri = lax.broadcasted_iota(jnp.int32, (2 * CHUNK, 2 * CHUNK), 0)
    ci = lax.broadcasted_iota(jnp.int32, (2 * CHUNK, 2 * CHUNK), 1)
    mask = ci % CHUNK < ri % CHUNK + ri // CHUNK
    e_r = lax.broadcasted_iota(jnp.int32, (CHUNK, CHUNK), 0)
    e_c = lax.broadcasted_iota(jnp.int32, (CHUNK, CHUNK), 1)
    eye = (e_r == e_c).astype(F32)
    zpad = jnp.zeros((CHUNK, hd), F32)

    def pre_body(cc, carry):
        cs = [cc * PRE_CHUNKS + u for u in range(PRE_CHUNKS)]
        ch = [(c, h) for c in cs for h in hs]
        idx = range(len(ch))
        ar = [ar_ref[h, c] for c, h in ch]
        g = [jnp.where(mask, _dot_nt(ar[n], bk_ref[h, c]), 0.0) for n, (c, h) in enumerate(ch)]
        gb = [x.astype(BF16) for x in g]
        gv = [_dot(gb[n], zv_ref[h, c]) for n, (c, h) in enumerate(ch)]
        a = [x[:CHUNK, :hd] for x in g]
        ab = [x.astype(BF16) for x in a]
        pw = [_dot(x, x) for x in ab]
        xs = [eye + x for x in a]
        for _ in range(4):
            res = [_dot(jnp.concatenate([xs[n], pw[n]], axis=0).astype(BF16), pw[n].astype(BF16))
                   for n in idx]
            xs = [xs[n] + res[n][:CHUNK] for n in idx]
            pw = [res[n][CHUNK:] for n in idx]
        tinv = [xs[n] + _dot(xs[n].astype(BF16), pw[n].astype(BF16)) for n in idx]
        e = [jnp.concatenate([ar[n][:CHUNK].astype(F32), zpad], axis=1) + gv[n][:CHUNK] for n in idx]
        wb = [_dot(tinv[n].astype(BF16), e[n].astype(BF16)).astype(BF16) for n in idx]
        for n, (c, h) in enumerate(ch):
            rows = pl.ds(pl.multiple_of(c * CHUNK, CHUNK), CHUNK)
            ry_ref[h, rows, :] = (jnp.concatenate([ar[n][CHUNK:].astype(F32), zpad], axis=1)
                                  + gv[n][CHUNK:] + _dot(gb[n][CHUNK:, :hd], wb[n]))
            left = jnp.concatenate([wb[n], zv_ref[h, c, CHUNK:, :]], axis=0)
            mn_ref[h, c] = _dot_tn(left, bkh_ref[h, c])
        return carry

    lax.fori_loop(0, nchunk // PRE_CHUNKS, pre_body, 0)

    def seq_body(c, carry):
        r0 = pl.multiple_of(c * CHUNK, CHUNK)
        rows = pl.ds(r0, CHUNK)
        s0 = [s_ref[h] for h in hs]
        s0b = [x.astype(BF16) for x in s0]
        ry = [ry_ref[h, rows, :] for h in hs]
        mn = [mn_ref[h, c] for h in hs]
        for h in hs:
            y_ref[h, rows, :] = ry[h] + _dot_nt(ry[h][:, :hd].astype(BF16), s0b[h])
            s_ref[h, CHUNK:, :] = (s0[h][CHUNK:] * gc_ref[h, pl.ds(r0, 1), :]
                                   + _dot(s0b[h][CHUNK:], mn[h][:CHUNK].astype(BF16)) + mn[h][CHUNK:])
        return carry

    lax.fori_loop(0, nchunk, seq_body, 0)

    y = jnp.concatenate([y_ref[h][:, hd:] for h in hs], axis=-1)
    mean = segsum(y) * (1.0 / hd)
    dlt = y - mean
    var = segsum(dlt * dlt) * (1.0 / hd)
    yn = dlt * lax.rsqrt(var + GN_EPS) * lg_ref[...] + lb_ref[...]
    o_ref[...] = ((yn + bonus_ref[...]) * gate_ref[...]).astype(o_ref.dtype)


def _rwkv(p, batch, seq, mu, w0, w2, a0, a2, g2, k_k, k_a, r_k, lnx_g, lnx_b):
    n, cols = p.shape
    width = w0.shape[0]
    heads = width // RWKV_HEAD
    assert RWKV_HEAD == CHUNK
    tt = TT_RWKV
    nt = seq // tt
    nchunk = tt // CHUNK
    wwa = jnp.zeros((DECAY_LORA + AAA_LORA, 2 * width), F32)
    wwa = wwa.at[:DECAY_LORA, :width].set(w2).at[DECAY_LORA:, width:].set(a2).astype(BF16)
    lane_head = jnp.arange(width) // RWKV_HEAD
    seg = (lane_head[:, None] == lane_head[None, :]).astype(BF16)
    tch = jnp.arange(tt) // CHUNK
    same = tch[:, None] == tch[None, :]
    tri = (same & (jnp.arange(tt)[None, :] <= jnp.arange(tt)[:, None])).astype(BF16)
    blk = same.astype(BF16)
    row = lambda a: a.reshape(1, -1)
    vec = lambda w: pl.BlockSpec((1, w), lambda b, t: (0, 0))
    full2 = lambda a: pl.BlockSpec(a.shape, lambda b, t: (0, 0))
    slab = lambda lanes, dt: pltpu.VMEM((heads, nchunk, 2 * CHUNK, lanes), dt)
    kern = functools.partial(_rwkv_kernel, width=width, heads=heads)
    return pl.pallas_call(
        kern,
        grid=(batch, nt),
        in_specs=[
            pl.BlockSpec((tt, cols), lambda b, t: (b * nt + t, 0)),
            vec(cols), vec(width), vec(width), full2(wwa), pl.BlockSpec(g2.shape, lambda b, t: (0, 0)),
            vec(width), vec(width), vec(width), vec(width), vec(width),
            full2(seg), full2(tri), full2(blk),
        ],
        out_specs=pl.BlockSpec((tt, width), lambda b, t: (b * nt + t, 0)),
        out_shape=jax.ShapeDtypeStruct((n, width), BF16),
        scratch_shapes=[
            pltpu.VMEM((heads, 2 * RWKV_HEAD, RWKV_HEAD), F32),
            pltpu.VMEM((8, cols), F32),
            slab(RWKV_HEAD, BF16),
            slab(RWKV_HEAD, BF16),
            slab(RWKV_HEAD, BF16),
            slab(2 * RWKV_HEAD, BF16),
            pltpu.VMEM((heads, tt, RWKV_HEAD), F32),
            pltpu.VMEM((heads, tt, 2 * RWKV_HEAD), F32),
            slab(RWKV_HEAD, F32),
            pltpu.VMEM((heads, tt, 2 * RWKV_HEAD), F32),
            pltpu.VMEM((tt, width), F32), pltpu.VMEM((tt, width), F32),
        ],
        compiler_params=_cparams(("arbitrary", "arbitrary")),
        name="rwkv",
    )(p, row(mu), row(w0), row(a0), wwa, g2.astype(BF16), row(k_k), row(k_a), row(r_k),
      row(lnx_g), row(lnx_b), seg, tri, blk)


def _diff_kernel(lq1_ref, lk1_ref, lq2_ref, lk2_ref, sg_ref, q_ref, k_ref, vt_ref, o_ref,
                 m0_ref, a0_ref, m1_ref, a1_ref, sa0_ref, sa1_ref, sb0_ref, sb1_ref,
                 ma0_ref, ma1_ref, mb0_ref, mb1_ref):
    i = pl.program_id(2)
    tq = q_ref.shape[0]
    tk = TK_ATT
    d = DIFF_HEAD
    hw = 2 * d
    nchunk = tk // CHUNK
    neg = -1e30
    qf = q_ref[...].astype(F32) * (d ** -0.5 * math.log2(math.e))
    lane = lax.broadcasted_iota(jnp.int32, qf.shape, 1)
    rchunk = lax.broadcasted_iota(jnp.int32, qf.shape, 0) // CHUNK
    q0 = jnp.where(lane < d, qf, 0.0).astype(BF16)
    q1 = jnp.where(lane >= d, qf, 0.0).astype(BF16)
    c0 = lane - d
    c1 = lane
    in0 = c0 * (nchunk - 1 - c0) >= 0
    in1 = c1 < nchunk
    qm0 = jnp.where(in0, jnp.where(rchunk < c0, neg, 0.0), 0.0).astype(BF16)
    qm1 = jnp.where(in1, jnp.where(rchunk < c1, neg, 0.0), 0.0).astype(BF16)
    keep0 = jnp.where(in0, 0.0, 1.0).astype(BF16)
    keep1 = jnp.where(in1, 0.0, 1.0).astype(BF16)
    hot0 = jnp.where(in0, jnp.where(rchunk == c0, 1.0, 0.0), 0.0).astype(BF16)
    hot1 = jnp.where(in1, jnp.where(rchunk == c1, 1.0, 0.0), 0.0).astype(BF16)

    for m_ref, a_ref in ((m0_ref, a0_ref), (m1_ref, a1_ref)):
        m_ref[...] = jnp.full(m_ref.shape, neg, F32)
        a_ref[...] = jnp.zeros(a_ref.shape, F32)

    def produce(bufs, j):
        s0_ref, s1_ref, x0_ref, x1_ref = bufs
        diag = jnp.where(j == i, 1.0, 0.0).astype(BF16)
        kb = k_ref[pl.ds(pl.multiple_of(j * tk, tk), tk), :]
        s0 = _dot_nt(kb * keep0 + hot0, q0 + qm0 * diag)
        s1 = _dot_nt(kb * keep1 + hot1, q1 + qm1 * diag)
        s0_ref[...] = s0
        s1_ref[...] = s1
        x0_ref[...] = jnp.max(s0, axis=0, keepdims=True)
        x1_ref[...] = jnp.max(s1, axis=0, keepdims=True)

    def update(st, smax, vt, m_ref, a_ref):
        m_old = m_ref[...]
        m_new = jnp.maximum(m_old, smax)
        alpha = jnp.exp2(m_old - m_new)
        pt = jnp.exp2(st - m_new).astype(BF16)
        a_ref[...] = alpha * a_ref[...] + _dot(vt, pt)
        m_ref[...] = m_new

    def consume(bufs, j):
        s0_ref, s1_ref, x0_ref, x1_ref = bufs
        vt = vt_ref[:, pl.ds(pl.multiple_of(j * tk, tk), tk)]
        update(s0_ref[...], x0_ref[...], vt, m0_ref, a0_ref)
        update(s1_ref[...], x1_ref[...], vt, m1_ref, a1_ref)

    buf_a = (sa0_ref, sa1_ref, ma0_ref, ma1_ref)
    buf_b = (sb0_ref, sb1_ref, mb0_ref, mb1_ref)
    nb = i + 1
    npairs = (nb - 1) // 2
    produce(buf_a, 0)

    def body(t, carry):
        j = 2 * t
        produce(buf_b, j + 1)
        consume(buf_a, j)
        produce(buf_a, j + 2)
        consume(buf_b, j + 1)
        return carry

    lax.fori_loop(0, npairs, body, 0)
    last = 2 * npairs

    @pl.when(last == i)
    def _():
        consume(buf_a, last)

    @pl.when(last < i)
    def _():
        produce(buf_b, last + 1)
        consume(buf_a, last)
        consume(buf_b, last + 1)

    lam = (jnp.exp(jnp.sum(lq1_ref[...] * lk1_ref[...], axis=-1, keepdims=True))
           - jnp.exp(jnp.sum(lq2_ref[...] * lk2_ref[...], axis=-1, keepdims=True)) + LAMBDA_INIT)
    ot = (a0_ref[:hw, :] / a0_ref[hw:hw + 1, :] - lam * (a1_ref[:hw, :] / a1_ref[hw:hw + 1, :]))
    o = _rms(ot.T, sg_ref[...]) * (1.0 - LAMBDA_INIT)
    o_ref[...] = o.astype(o_ref.dtype)


def _diffattn(qk, vt, batch, seq, lq1, lk1, lq2, lk2, subln_g):
    n = qk.shape[0]
    width = qk.shape[1] // 2
    hw = 2 * DIFF_HEAD
    heads = width // hw
    tq = TQ_ATT
    assert TK_ATT == tq and seq % tq == 0
    nq = seq // tq
    vec = lambda w: pl.BlockSpec((1, w), lambda b, h, i: (0, 0))
    row = lambda a: a.reshape(1, -1)
    stat = pltpu.VMEM((1, tq), F32)
    acc = pltpu.VMEM((VT_ROWS, tq), F32)
    return pl.pallas_call(
        _diff_kernel,
        grid=(batch, heads, nq),
        in_specs=[
            vec(DIFF_HEAD), vec(DIFF_HEAD), vec(DIFF_HEAD), vec(DIFF_HEAD), vec(hw),
            pl.BlockSpec((tq, hw), lambda b, h, i: (b * nq + i, h)),
            pl.BlockSpec((seq, hw), lambda b, h, i: (b, heads + h)),
            pl.BlockSpec((VT_ROWS, seq), lambda b, h, i: (h, b)),
        ],
        out_specs=pl.BlockSpec((tq, hw), lambda b, h, i: (b * nq + i, h)),
        out_shape=jax.ShapeDtypeStruct((n, width), BF16),
        scratch_shapes=[stat, acc, stat, acc] + [pltpu.VMEM((TK_ATT, tq), F32)] * 4 + [stat] * 4,
        compiler_params=_cparams(("arbitrary", "arbitrary", "arbitrary")),
        name="diffattn",
    )(row(lq1), row(lk1), row(lq2), row(lk2), row(subln_g), qk, qk, vt)


def _memkv_kernel(m_ref, g_ref, w_ref, o_ref):
    o_ref[...] = _dot(_rms(m_ref[...], g_ref[...]).astype(BF16), w_ref[...]).astype(o_ref.dtype)


def _memkv(mem2, g, wkv):
    n, d = mem2.shape
    c = wkv.shape[1]
    tn = 512
    return pl.pallas_call(
        _memkv_kernel,
        grid=(c // tn,),
        in_specs=[
            pl.BlockSpec((n, d), lambda j: (0, 0)),
            pl.BlockSpec((1, d), lambda j: (0, 0)),
            pl.BlockSpec((d, tn), lambda j: (0, j)),
        ],
        out_specs=pl.BlockSpec((n, tn), lambda j: (0, j)),
        out_shape=jax.ShapeDtypeStruct((n, c), BF16),
        compiler_params=_cparams(("arbitrary",)),
        name="memkv",
    )(mem2, g.reshape(1, d), wkv.astype(BF16))


def _mid_kernel(x_ref, yr_ref, yd_ref, wo1_ref, wo2_ref, g2_ref, wq_ref, km_ref, vm_ref, wox_ref,
                g3_ref, rw_ref, rb_ref, tril_ref,
                h_ref, xn_ref, ti_ref, gt_ref, rk_ref, cnt_ref, base_ref):
    step = pl.program_id(0) * pl.num_programs(1) + pl.program_id(1)
    tm, dm = x_ref.shape
    hd = dm // XATTN_HEADS

    @pl.when(step == 0)
    def _():
        base_ref[...] = jnp.zeros(base_ref.shape, F32)

    h1 = x_ref[...] + _dot(yr_ref[...], wo1_ref[...]) + _dot(yd_ref[...], wo2_ref[...])
    hn = _rms(h1, g2_ref[...]).astype(BF16)
    q = (_dot(hn, wq_ref[...]) * (hd ** -0.5)).astype(BF16)
    outs = []
    for hh in range(XATTN_HEADS):
        sl = slice(hh * hd, (hh + 1) * hd)
        s = _dot_nt(q[:, sl], km_ref[:, sl])
        s = s - jnp.max(s, axis=-1, keepdims=True)
        e = jnp.exp(s)
        pr = e / jnp.sum(e, axis=-1, keepdims=True)
        outs.append(_dot(pr.astype(BF16), vm_ref[:, sl]))
    o = jnp.concatenate(outs, axis=-1).astype(BF16)
    h2 = h1 + _dot(o, wox_ref[...])
    h_ref[...] = h2
    xn = _rms(h2, g3_ref[...])
    _store_rows(xn_ref, xn)

    x_hi, x_lo = _split2(xn)
    logits = _dot(x_hi, rw_ref[0]) + (_dot(x_hi, rw_ref[1]) + _dot(x_lo, rw_ref[0])) + rb_ref[...]

    ne = logits.shape[1]
    eio = lax.broadcasted_iota(jnp.int32, (tm, ne), 1).astype(F32)
    work = logits
    vals, idxs, sels = [], [], []
    for _ in range(TOP_K):
        mx = jnp.max(work, axis=-1, keepdims=True)
        ix = jnp.min(jnp.where(work == mx, eio, float(ne)), axis=-1, keepdims=True)
        sel = eio == ix
        vals.append(mx)
        idxs.append(ix)
        sels.append(sel)
        work = jnp.where(sel, -jnp.inf, work)
    es = [jnp.exp(vv - vals[0]) for vv in vals]
    den = es[0] + es[1] + es[2] + es[3]
    gates = [ee / den for ee in es]

    assign = jnp.zeros((tm, ne), F32)
    for sel in sels:
        assign = assign + sel.astype(F32)
    before = _dot(tril_ref[...], assign.astype(BF16)) + base_ref[...]
    ranks = [jnp.sum(jnp.where(sel, before, 0.0), axis=-1, keepdims=True) for sel in sels]
    base_new = base_ref[...] + jnp.sum(assign, axis=0, keepdims=True)
    base_ref[...] = base_new
    cnt_ref[...] = base_new.astype(jnp.int32)

    kio = lax.broadcasted_iota(jnp.int32, (tm, TOP_K), 1)

    def cols(parts):
        out = jnp.where(kio == 0, parts[0], parts[1])
        for kk in range(2, TOP_K):
            out = jnp.where(kio == kk, parts[kk], out)
        return out

    ti_ref[...] = cols(idxs).astype(jnp.int32)
    gt_ref[...] = cols(gates)
    rk_ref[...] = cols(ranks).astype(jnp.int32)


def _mid(x2, yr, yd, batch, seq, w_out, norm2_g, wq, kv, wo, norm3_g, router_w, router_b):
    n, d = x2.shape
    tm = TM_MID
    nt = seq // tm
    mem_len = kv.shape[0] // batch
    wr = yr.shape[1]
    wo1 = w_out[:wr].astype(BF16)
    wo2 = w_out[wr:].astype(BF16)
    rw = jnp.stack(_split2(router_w))
    ne = router_w.shape[1]
    tril = (jnp.arange(tm)[None, :] < jnp.arange(tm)[:, None]).astype(BF16)
    tok = lambda w: pl.BlockSpec((tm, w), lambda b, t: (b * nt + t, 0))
    cst = lambda a: pl.BlockSpec(a.shape, lambda b, t: (0,) * a.ndim)
    row = lambda a: a.reshape(1, -1)
    return pl.pallas_call(
        _mid_kernel,
        grid=(batch, nt),
        in_specs=[
            tok(d), tok(wr), tok(d - wr), cst(wo1), cst(wo2), cst(row(norm2_g)), cst(wq),
            pl.BlockSpec((mem_len, d), lambda b, t: (b, 0)),
            pl.BlockSpec((mem_len, d), lambda b, t: (b, 1)),
            cst(wo), cst(row(norm3_g)), cst(rw), cst(row(router_b)), cst(tril),
        ],
        out_specs=[tok(d), pl.BlockSpec((tm * ROW_TILE, LANES), lambda b, t: (b * nt + t, 0)),
                   tok(TOP_K), tok(TOP_K), tok(TOP_K),
                   pl.BlockSpec((1, ne), lambda b, t: (0, 0))],
        out_shape=[
            jax.ShapeDtypeStruct((n, d), F32),
            jax.ShapeDtypeStruct((n * ROW_TILE, LANES), F32),
            jax.ShapeDtypeStruct((n, TOP_K), jnp.int32),
            jax.ShapeDtypeStruct((n, TOP_K), F32),
            jax.ShapeDtypeStruct((n, TOP_K), jnp.int32),
            jax.ShapeDtypeStruct((1, ne), jnp.int32),
        ],
        scratch_shapes=[pltpu.VMEM((1, ne), F32)],
        compiler_params=_cparams(("arbitrary", "arbitrary")),
        name="mid",
    )(x2, yr, yd, wo1, wo2, row(norm2_g), wq.astype(BF16), kv, kv, wo.astype(BF16),
      row(norm3_g), rw, row(router_b), tril)


def _dispatch_kernel(dest_ref, pad_start_ref, pad_len_ref, x_ref, o_ref, zero_ref, sem, zsem):
    i = pl.program_id(0)
    tm = x_ref.shape[0] // ROW_TILE

    @pl.when(i == 0)
    def _():
        zero_ref[...] = jnp.zeros(zero_ref.shape, zero_ref.dtype)
        pieces = [MOE_TILE >> (s + 1) for s in range(MOE_TILE.bit_length() - 1)]

        def zcopy(nrows, dst_row):
            return pltpu.make_async_copy(
                zero_ref.at[pl.ds(0, nrows * ROW_TILE), :],
                o_ref.at[pl.ds(pl.multiple_of(dst_row * ROW_TILE, ROW_TILE), nrows * ROW_TILE), :], zsem)

        def per_expert(e, carry):
            row = pad_start_ref[e]
            left = pad_len_ref[e]
            for piece in pieces:
                hit = (left & piece) != 0

                @pl.when(hit)
                def _():
                    zcopy(piece, row).start()

                row = row + jnp.where(hit, piece, 0)
            row = pad_start_ref[e]
            for piece in pieces:
                hit = (left & piece) != 0

                @pl.when(hit)
                def _():
                    zcopy(piece, row).wait()

                row = row + jnp.where(hit, piece, 0)
            return carry

        lax.fori_loop(0, N_EXPERTS, per_expert, 0)

        half = MOE_TILE // 2
        tail_start = pad_start_ref[N_EXPERTS]
        tail_pieces = pad_len_ref[N_EXPERTS] // half

        def tail_start_copy(t, carry):
            zcopy(half, tail_start + t * half).start()
            return carry

        def tail_wait_copy(t, carry):
            zcopy(half, tail_start).wait()
            return carry

        lax.fori_loop(0, tail_pieces, tail_start_copy, 0)
        lax.fori_loop(0, tail_pieces, tail_wait_copy, 0)

    def copy(r8, dst8):
        return pltpu.make_async_copy(x_ref.at[pl.ds(pl.multiple_of(r8, ROW_TILE), ROW_TILE), :],
                                     o_ref.at[pl.ds(pl.multiple_of(dst8, ROW_TILE), ROW_TILE), :], sem)

    def start(r, carry):
        for kk in range(TOP_K):
            copy(r * ROW_TILE, dest_ref[(i * tm + r) * TOP_K + kk]).start(priority=kk % 2)
        return carry

    lax.fori_loop(0, tm, start, 0, unroll=DMA_UNROLL)

    def wait(r, carry):
        for kk in range(TOP_K):
            copy(0, 0).wait()
        return carry

    lax.fori_loop(0, tm, wait, 0, unroll=WAIT_UNROLL)


def _dispatch(dest8, pad_start, pad_len, xn, rows_padded):
    tm = TM_DISP
    n = xn.shape[0] // ROW_TILE
    return pl.pallas_call(
        _dispatch_kernel,
        grid_spec=pltpu.PrefetchScalarGridSpec(
            num_scalar_prefetch=3,
            grid=(n // tm,),
            in_specs=[pl.BlockSpec((tm * ROW_TILE, LANES), lambda i, *_: (i, 0))],
            out_specs=pl.BlockSpec(memory_space=pl.ANY),
            scratch_shapes=[pltpu.VMEM((MOE_TILE // 2 * ROW_TILE, LANES), xn.dtype),
                            pltpu.SemaphoreType.DMA(()), pltpu.SemaphoreType.DMA(())],
        ),
        out_shape=jax.ShapeDtypeStruct((rows_padded * ROW_TILE, LANES), xn.dtype),
        compiler_params=_cparams(("arbitrary",)),
        name="dispatch",
    )(dest8, pad_start, pad_len, xn)


def _experts_kernel(be_ref, nu_ref, x_ref, w1_ref, b1_ref, w2_ref, b2_ref, o_ref,
                    w1b_ref, w2b_ref, prev_ref):
    j = pl.program_id(0)
    e = be_ref[j]
    ff = w2_ref.shape[0]

    @pl.when(j == 0)
    def _():
        prev_ref[0] = -1

    @pl.when(j < nu_ref[0])
    def _():
        @pl.when(e != prev_ref[0])
        def _():
            w1b_ref[...] = w1_ref[...].astype(BF16)
            w2b_ref[...] = w2_ref[...].astype(BF16)
            prev_ref[0] = e

        h = _dot(_load_rows(x_ref, MOE_TILE).astype(BF16), w1b_ref[...]) + b1_ref[...]
        gate = jnp.minimum(h[:, :ff], SWIGLU_LIMIT)
        lin = jnp.clip(h[:, ff:], -SWIGLU_LIMIT, SWIGLU_LIMIT)
        act = gate * jax.nn.sigmoid(SWIGLU_ALPHA * gate) * (lin + 1.0)
        _store_rows(o_ref, _dot(act.astype(BF16), w2b_ref[...]) + b2_ref[...])

    @pl.when(j >= nu_ref[0])
    def _():
        o_ref[...] = jnp.zeros(o_ref.shape, F32)


def _experts(block_e, nused, xin, w1, b1, w2, b2):
    ne, d, f2 = w1.shape
    rows = xin.shape[0] // ROW_TILE
    ff = w2.shape[1]
    nblk = rows // MOE_TILE
    return pl.pallas_call(
        _experts_kernel,
        grid_spec=pltpu.PrefetchScalarGridSpec(
            num_scalar_prefetch=2,
            grid=(nblk,),
            in_specs=[
                pl.BlockSpec((MOE_TILE * ROW_TILE, LANES), lambda j, be, nu: (jnp.minimum(j, nu[0] - 1), 0)),
                pl.BlockSpec((None, d, f2), lambda j, be, nu: (be[j], 0, 0)),
                pl.BlockSpec((None, 1, f2), lambda j, be, nu: (be[j], 0, 0)),
                pl.BlockSpec((None, ff, d), lambda j, be, nu: (be[j], 0, 0)),
                pl.BlockSpec((None, 1, d), lambda j, be, nu: (be[j], 0, 0)),
            ],
            out_specs=pl.BlockSpec((MOE_TILE * ROW_TILE, LANES), lambda j, be, nu: (j, 0)),
            scratch_shapes=[pltpu.VMEM((d, f2), BF16), pltpu.VMEM((ff, d), BF16),
                            pltpu.SMEM((1,), jnp.int32)],
        ),
        out_shape=jax.ShapeDtypeStruct(xin.shape, F32),
        compiler_params=_cparams(("arbitrary",)),
        name="experts",
    )(block_e, nused, xin, w1, b1.reshape(ne, 1, f2), w2, b2.reshape(ne, 1, d))


def _combine_kernel(dest_ref, h_ref, gt_ref, g_ref, yb_ref, o_ref, buf_ref, sem):
    i = pl.program_id(0)
    tm = h_ref.shape[0]
    slot = i % 2

    def copy(src8, dst_slot, kk, r8):
        return pltpu.make_async_copy(
            yb_ref.at[pl.ds(pl.multiple_of(src8, ROW_TILE), ROW_TILE), :],
            buf_ref.at[dst_slot, kk, pl.ds(pl.multiple_of(r8, ROW_TILE), ROW_TILE), :], sem.at[dst_slot])

    def issue(tile, dst_slot):
        def start(r, carry):
            for kk in range(TOP_K):
                copy(dest_ref[(tile * tm + r) * TOP_K + kk], dst_slot, kk, r * ROW_TILE).start(priority=kk % 2)
            return carry

        lax.fori_loop(0, tm, start, 0, unroll=DMA_UNROLL)

    @pl.when(i == 0)
    def _():
        issue(0, 0)

    @pl.when(i + 1 < pl.num_programs(0))
    def _():
        issue(i + 1, 1 - slot)

    def wait(r, carry):
        for kk in range(TOP_K):
            copy(0, slot, kk, 0).wait()
        return carry

    lax.fori_loop(0, tm, wait, 0, unroll=WAIT_UNROLL)

    gt = gt_ref[...]
    rg = COMB_ROWS
    for g0 in range(0, tm, rg):
        rows = slice(g0, g0 + rg)
        parts = []
        for j in range(ROW_TILE):
            y = h_ref[rows, j * LANES:(j + 1) * LANES]
            for kk in range(TOP_K):
                y = y + gt[rows, kk:kk + 1] * buf_ref[slot, kk, pl.ds(g0 * ROW_TILE + j, rg, stride=ROW_TILE), :]
            parts.append(y)
        y = jnp.concatenate(parts, axis=1)
        o_ref[rows, :] = _rms(y, g_ref[...])


def _combine(dest_flat, h2, gates, final_g, yb):
    n, d = h2.shape
    tm = TM_COMB
    return pl.pallas_call(
        _combine_kernel,
        grid_spec=pltpu.PrefetchScalarGridSpec(
            num_scalar_prefetch=1,
            grid=(n // tm,),
            in_specs=[
                pl.BlockSpec((tm, d), lambda i, dest: (i, 0)),
                pl.BlockSpec((tm, TOP_K), lambda i, dest: (i, 0)),
                pl.BlockSpec((1, d), lambda i, dest: (0, 0)),
                pl.BlockSpec(memory_space=pl.ANY),
            ],
            out_specs=pl.BlockSpec((tm, d), lambda i, dest: (i, 0)),
            scratch_shapes=[pltpu.VMEM((2, TOP_K, tm * ROW_TILE, LANES), F32), pltpu.SemaphoreType.DMA((2,))],
        ),
        out_shape=jax.ShapeDtypeStruct((n, d), F32),
        compiler_params=_cparams(("arbitrary",)),
        name="combine",
    )(dest_flat, h2, gates, final_g.reshape(1, d), yb)


def kernel(x, mem, norm1_g, w_in, mu_shift, w0, w2, a0, a2, g2, k_k, k_a, r_k, lnx_g, lnx_b,
           lam_q1, lam_k1, lam_q2, lam_k2, subln_g, w_out, norm2_g, mem_norm_g, wq_x, wkv_x, wo_x,
           norm3_g, router_w, router_b, moe_w1, moe_b1, moe_w2, moe_b2, final_g):
    batch, seq, d = x.shape
    n = batch * seq
    shift_cols = mu_shift.shape[1]
    x2 = x.reshape(n, d)

    p, qk, vt = _inproj(x2, norm1_g[0], w_in[0], shift_cols)
    y_rwkv = _rwkv(p, batch, seq, mu_shift[0], w0[0], w2[0], a0[0], a2[0], g2[0], k_k[0], k_a[0],
                   r_k[0].reshape(-1), lnx_g[0], lnx_b[0])
    y_diff = _diffattn(qk, vt, batch, seq, lam_q1[0], lam_k1[0], lam_q2[0], lam_k2[0], subln_g[0])
    kv = _memkv(mem.reshape(-1, d), mem_norm_g[0], wkv_x[0])

    h2, xn3, top_i, gates, rank, counts = _mid(
        x2, y_rwkv, y_diff, batch, seq, w_out[0], norm2_g[0], wq_x[0], kv, wo_x[0], norm3_g[0],
        router_w[0], router_b[0])

    counts = counts.reshape(-1)
    padded = ((counts + MOE_TILE - 1) // MOE_TILE) * MOE_TILE
    padded_end = jnp.cumsum(padded)
    padded_off = padded_end - padded
    rows_padded = n * TOP_K + N_EXPERTS * MOE_TILE
    nblk = rows_padded // MOE_TILE
    block_start = jnp.arange(nblk, dtype=jnp.int32) * MOE_TILE
    block_e = jnp.minimum(jnp.sum(block_start[:, None] >= padded_end[None, :], axis=1),
                          N_EXPERTS - 1).astype(jnp.int32)
    nused = (padded_end[-1:] // MOE_TILE).astype(jnp.int32)
    expert_hot = top_i[:, :, None] == jnp.arange(N_EXPERTS, dtype=jnp.int32)
    row_off = jnp.sum(jnp.where(expert_hot, padded_off.astype(jnp.int32), 0), axis=-1)
    dest8 = ((row_off + rank) * ROW_TILE).astype(jnp.int32).reshape(-1)
    pad_start = jnp.concatenate([padded_off + counts, padded_end[-1:]]).astype(jnp.int32)
    pad_len = jnp.concatenate([padded - counts, rows_padded - padded_end[-1:]]).astype(jnp.int32)

    xin = _dispatch(dest8, pad_start, pad_len, xn3, rows_padded)
    yb = _experts(block_e, nused, xin, moe_w1[0], moe_b1[0], moe_w2[0], moe_b2[0])
    out = _combine(dest8, h2, gates, final_g, yb)
    return out.reshape(batch, seq, d)
```

```python
import functools
import math

import jax
import jax.numpy as jnp
from jax import lax
from jax.experimental import pallas as pl
from jax.experimental.pallas import tpu as pltpu

F32 = jnp.float32
BF16 = jnp.bfloat16

CHUNK = 64
RWKV_HEAD = 64
DECAY_LORA = 64
AAA_LORA = 64
GATE_LORA = 128
DIFF_HEAD = 64
XATTN_HEADS = 4
N_EXPERTS = 32
TOP_K = 4
SWIGLU_LIMIT = 7.0
SWIGLU_ALPHA = 1.702
NORM_EPS = 1e-5
GN_EPS = 64e-5
LAMBDA_INIT = 0.8 - 0.6 * math.exp(-0.3 * 0)

VMEM_LIMIT = 56 * 1024 * 1024
TM_PROJ = 512
TT_RWKV = 256
PRE_CHUNKS = 4
VT_ROWS = 144
TQ_ATT = 512
TK_ATT = 512
TM_MID = 512
TM_DISP = 512
TM_COMB = 512
MOE_TILE = 512
DMA_UNROLL = 4
COMB_ROWS = 32
WAIT_UNROLL = 32
ROW_TILE = 8
LANES = 128


def _cparams(sem):
    return pltpu.CompilerParams(dimension_semantics=sem, vmem_limit_bytes=VMEM_LIMIT)


def _dot(a, b):
    return jnp.dot(a, b, preferred_element_type=F32)


def _dot_nt(a, b):
    return lax.dot_general(a, b, (((1,), (1,)), ((), ())), preferred_element_type=F32)


def _dot_tn(a, b):
    return lax.dot_general(a, b, (((0,), (0,)), ((), ())), preferred_element_type=F32)


def _rms(x, g):
    return x * lax.rsqrt(jnp.mean(x * x, axis=-1, keepdims=True) + NORM_EPS) * g


def _load_rows(ref, nrows):
    return jnp.concatenate([ref[pl.ds(j, nrows, stride=ROW_TILE), :] for j in range(ROW_TILE)], axis=1)


def _store_rows(ref, val):
    nrows = val.shape[0]
    for j in range(ROW_TILE):
        ref[pl.ds(j, nrows, stride=ROW_TILE), :] = val[:, j * LANES:(j + 1) * LANES]


def _split2(x):
    hi = x.astype(BF16)
    lo = (x - hi.astype(F32)).astype(BF16)
    return hi, lo


def _split3(x):
    hi = x.astype(BF16)
    r1 = x - hi.astype(F32)
    mid = r1.astype(BF16)
    lo = (r1 - mid.astype(F32)).astype(BF16)
    return hi, mid, lo


def _inproj_kernel(x_ref, g_ref, wa_ref, wb_ref, wvt_ref, p_ref, qk_ref, vt_ref):
    xn = _rms(x_ref[...], g_ref[...]).astype(BF16)
    p_ref[...] = _dot(xn, wa_ref[...])
    qk_ref[...] = _dot(xn, wb_ref[...]).astype(BF16)
    vt = _dot_nt(wvt_ref[...], xn)
    hw = 2 * DIFF_HEAD
    ones = jnp.ones((VT_ROWS - hw, vt.shape[1]), F32)
    for h in range(vt.shape[0] // hw):
        vt_ref[h * VT_ROWS:(h + 1) * VT_ROWS, :] = jnp.concatenate(
            [vt[h * hw:(h + 1) * hw], ones], axis=0).astype(BF16)


def _inproj(x2, g, w_in, shift_cols):
    n, d = x2.shape
    dw = (w_in.shape[1] - shift_cols) // 3
    wa = w_in[:, :shift_cols].astype(BF16)
    wb = w_in[:, shift_cols:shift_cols + 2 * dw].astype(BF16)
    wvt = w_in[:, shift_cols + 2 * dw:].T.astype(BF16)
    tm = TM_PROJ
    return pl.pallas_call(
        _inproj_kernel,
        grid=(n // tm,),
        in_specs=[
            pl.BlockSpec((tm, d), lambda i: (i, 0)),
            pl.BlockSpec((1, d), lambda i: (0, 0)),
            pl.BlockSpec((d, shift_cols), lambda i: (0, 0)),
            pl.BlockSpec((d, 2 * dw), lambda i: (0, 0)),
            pl.BlockSpec((dw, d), lambda i: (0, 0)),
        ],
        out_specs=[
            pl.BlockSpec((tm, shift_cols), lambda i: (i, 0)),
            pl.BlockSpec((tm, 2 * dw), lambda i: (i, 0)),
            pl.BlockSpec((dw // (2 * DIFF_HEAD) * VT_ROWS, tm), lambda i: (0, i)),
        ],
        out_shape=[
            jax.ShapeDtypeStruct((n, shift_cols), F32),
            jax.ShapeDtypeStruct((n, 2 * dw), BF16),
            jax.ShapeDtypeStruct((dw // (2 * DIFF_HEAD) * VT_ROWS, n), BF16),
        ],
        compiler_params=_cparams(("arbitrary",)),
        name="inproj",
    )(x2, g.reshape(1, d), wa, wb, wvt)


def _rwkv_kernel(p_ref, mu_ref, w0_ref, a0_ref, wwa_ref, g2_ref, kk_ref, ka_ref, rk_ref,
                 lg_ref, lb_ref, seg_ref, tri_ref, blk_ref, o_ref,
                 s_ref, last_ref, ar_ref, bk_ref, bkh_ref, zv_ref, gc_ref, ry_ref, mn_ref, y_ref,
                 bonus_ref, gate_ref, *, width, heads):
    t_idx = pl.program_id(1)
    tt = p_ref.shape[0]
    nchunk = tt // CHUNK
    hd = RWKV_HEAD
    hs = range(heads)

    @pl.when(t_idx == 0)
    def _():
        s_ref[...] = jnp.zeros(s_ref.shape, F32)
        last_ref[...] = jnp.zeros(last_ref.shape, F32)
        zv_ref[...] = jnp.zeros(zv_ref.shape, BF16)

    p = p_ref[...]
    prev = pltpu.roll(p, shift=1, axis=0)
    row0 = lax.broadcasted_iota(jnp.int32, (tt, 1), 0) == 0
    prev = jnp.where(row0, last_ref[0:1, :], prev)
    last_ref[0:1, :] = p[tt - 1:tt, :]
    pm = p + (prev - p) * mu_ref[...]

    r = pm[:, 0:width]
    k = pm[:, width:2 * width]
    v = pm[:, 2 * width:3 * width]
    wa = pm[:, 3 * width:3 * width + DECAY_LORA + AAA_LORA]
    gd = pm[:, 3 * width + DECAY_LORA + AAA_LORA:]

    lane = lax.broadcasted_iota(jnp.int32, wa.shape, 1)
    z = jnp.where(lane < DECAY_LORA, jnp.tanh(wa), wa).astype(BF16)
    lora = _dot(z, wwa_ref[...])
    logw = -math.exp(-0.5) * jax.nn.sigmoid(w0_ref[...] + lora[:, :width])
    lr = jax.nn.sigmoid(a0_ref[...] + lora[:, width:])
    gate_ref[...] = _dot(jax.nn.sigmoid(gd).astype(BF16), g2_ref[...])

    seg = seg_ref[...]

    def segsum(x):
        hi, lo = _split2(x)
        return _dot(hi, seg) + _dot(lo, seg)

    kk = k * kk_ref[...]
    kk = kk / jnp.maximum(jnp.sqrt(segsum(kk * kk)), 1e-12)
    k2 = k * (1.0 + (lr - 1.0) * ka_ref[...])
    bonus_ref[...] = segsum(r * k2 * rk_ref[...]) * v

    h3 = _split3(logw)
    tri = tri_ref[...]
    blk = blk_ref[...]
    cum = _dot(tri, h3[0]) + _dot(tri, h3[1]) + _dot(tri, h3[2])
    tot = _dot(blk, h3[0]) + _dot(blk, h3[1]) + _dot(blk, h3[2])
    einv = jnp.exp(-cum)
    ed = jnp.exp(tot - cum)
    b = kk * lr
    parts = (
        (-kk * jnp.exp(cum - logw), ar_ref, 0), (r * jnp.exp(cum), ar_ref, CHUNK),
        (b * einv, bk_ref, 0), (k2 * einv, bk_ref, CHUNK),
        (b * ed, bkh_ref, 0), (k2 * ed, bkh_ref, CHUNK),
    )
    gc = jnp.exp(tot)
    zlane = jnp.zeros((tt, hd), F32)
    for h in hs:
        sl = slice(h * hd, (h + 1) * hd)
        gc_ref[h] = gc[:, sl]
        zv = jnp.concatenate([zlane, v[:, sl]], axis=1).astype(BF16)
        for c in range(nchunk):
            zv_ref[h, c, CHUNK:, :] = zv[c * CHUNK:(c + 1) * CHUNK]
        for val, ref, off in parts:
            vb = val[:, sl].astype(BF16)
            for c in range(nchunk):
                ref[h, c, off:off + CHUNK, :] = vb[c * CHUNK:(c + 1) * CHUNK]

    ri = lax.broadcasted_iota(jnp.int32, (2 * CHUNK, 2 * CHUNK), 0)
    ci = lax.broadcasted_iota(jnp.int32, (2 * CHUNK, 2 * CHUNK), 1)
    mask = ci % CHUNK < ri % CHUNK + ri // CHUNK
    e_r = lax.broadcasted_iota(jnp.int32, (CHUNK, CHUNK), 0)
    e_c = lax.broadcasted_iota(jnp.int32, (CHUNK, CHUNK), 1)
    eye = (e_r == e_c).astype(F32)
    zpad = jnp.zeros((CHUNK, hd), F32)

    def pre_body(cc, carry):
        cs = [cc * PRE_CHUNKS + u for u in range(PRE_CHUNKS)]
        ch = [(c, h) for c in cs for h in hs]
        idx = range(len(ch))
        ar = [ar_ref[h, c] for c, h in ch]
        g = [jnp.where(mask, _dot_nt(ar[n], bk_ref[h, c]), 0.0) for n, (c, h) in enumerate(ch)]
        gb = [x.astype(BF16) for x in g]
        gv = [_dot(gb[n], zv_ref[h, c]) for n, (c, h) in enumerate(ch)]
        a = [x[:CHUNK, :hd] for x in g]
        ab = [x.astype(BF16) for x in a]
        pw = [_dot(x, x) for x in ab]
        xs = [eye + x for x in a]
        for _ in range(4):
            res = [_dot(jnp.concatenate([xs[n], pw[n]], axis=0).astype(BF16), pw[n].astype(BF16))
                   for n in idx]
            xs = [xs[n] + res[n][:CHUNK] for n in idx]
            pw = [res[n][CHUNK:] for n in idx]
        tinv = [xs[n] + _dot(xs[n].astype(BF16), pw[n].astype(BF16)) for n in idx]
        e = [jnp.concatenate([ar[n][:CHUNK].astype(F32), zpad], axis=1) + gv[n][:CHUNK] for n in idx]
        wb = [_dot(tinv[n].astype(BF16), e[n].astype(BF16)).astype(BF16) for n in idx]
        for n, (c, h) in enumerate(ch):
            rows = pl.ds(pl.multiple_of(c * CHUNK, CHUNK), CHUNK)
            ry_ref[h, rows, :] = (jnp.concatenate([ar[n][CHUNK:].astype(F32), zpad], axis=1)
                                  + gv[n][CHUNK:] + _dot(gb[n][CHUNK:, :hd], wb[n]))
            left = jnp.concatenate([wb[n], zv_ref[h, c, CHUNK:, :]], axis=0)
            mn_ref[h, c] = _dot_tn(left, bkh_ref[h, c])
        return carry

    lax.fori_loop(0, nchunk // PRE_CHUNKS, pre_body, 0)

    def seq_body(c, carry):
        r0 = pl.multiple_of(c * CHUNK, CHUNK)
        rows = pl.ds(r0, CHUNK)
        s0 = [s_ref[h] for h in hs]
        s0b = [x.astype(BF16) for x in s0]
        ry = [ry_ref[h, rows, :] for h in hs]
        mn = [mn_ref[h, c] for h in hs]
        for h in hs:
            y_ref[h, rows, :] = ry[h] + _dot_nt(ry[h][:, :hd].astype(BF16), s0b[h])
            s_ref[h, CHUNK:, :] = (s0[h][CHUNK:] * gc_ref[h, pl.ds(r0, 1), :]
                                   + _dot(s0b[h][CHUNK:], mn[h][:CHUNK].astype(BF16)) + mn[h][CHUNK:])
        return carry

    lax.fori_loop(0, nchunk, seq_body, 0)

    y = jnp.concatenate([y_ref[h][:, hd:] for h in hs], axis=-1)
    mean = segsum(y) * (1.0 / hd)
    dlt = y - mean
    var = segsum(dlt * dlt) * (1.0 / hd)
    yn = dlt * lax.rsqrt(var + GN_EPS) * lg_ref[...] + lb_ref[...]
    o_ref[...] = ((yn + bonus_ref[...]) * gate_ref[...]).astype(o_ref.dtype)


def _rwkv(p, batch, seq, mu, w0, w2, a0, a2, g2, k_k, k_a, r_k, lnx_g, lnx_b):
    n, cols = p.shape
    width = w0.shape[0]
    heads = width // RWKV_HEAD
    assert RWKV_HEAD == CHUNK
    tt = TT_RWKV
    nt = seq // tt
    nchunk = tt // CHUNK
    wwa = jnp.zeros((DECAY_LORA + AAA_LORA, 2 * width), F32)
    wwa = wwa.at[:DECAY_LORA, :width].set(w2).at[DECAY_LORA:, width:].set(a2).astype(BF16)
    lane_head = jnp.arange(width) // RWKV_HEAD
    seg = (lane_head[:, None] == lane_head[None, :]).astype(BF16)
    tch = jnp.arange(tt) // CHUNK
    same = tch[:, None] == tch[None, :]
    tri = (same & (jnp.arange(tt)[None, :] <= jnp.arange(tt)[:, None])).astype(BF16)
    blk = same.astype(BF16)
    row = lambda a: a.reshape(1, -1)
    vec = lambda w: pl.BlockSpec((1, w), lambda b, t: (0, 0))
    full2 = lambda a: pl.BlockSpec(a.shape, lambda b, t: (0, 0))
    slab = lambda lanes, dt: pltpu.VMEM((heads, nchunk, 2 * CHUNK, lanes), dt)
    kern = functools.partial(_rwkv_kernel, width=width, heads=heads)
    return pl.pallas_call(
        kern,
        grid=(batch, nt),
        in_specs=[
            pl.BlockSpec((tt, cols), lambda b, t: (b * nt + t, 0)),
            vec(cols), vec(width), vec(width), full2(wwa), pl.BlockSpec(g2.shape, lambda b, t: (0, 0)),
            vec(width), vec(width), vec(width), vec(width), vec(width),
            full2(seg), full2(tri), full2(blk),
        ],
        out_specs=pl.BlockSpec((tt, width), lambda b, t: (b * nt + t, 0)),
        out_shape=jax.ShapeDtypeStruct((n, width), BF16),
        scratch_shapes=[
            pltpu.VMEM((heads, 2 * RWKV_HEAD, RWKV_HEAD), F32),
            pltpu.VMEM((8, cols), F32),
            slab(RWKV_HEAD, BF16),
            slab(RWKV_HEAD, BF16),
            slab(RWKV_HEAD, BF16),
            slab(2 * RWKV_HEAD, BF16),
            pltpu.VMEM((heads, tt, RWKV_HEAD), F32),
            pltpu.VMEM((heads, tt, 2 * RWKV_HEAD), F32),
            slab(RWKV_HEAD, F32),
            pltpu.VMEM((heads, tt, 2 * RWKV_HEAD), F32),
            pltpu.VMEM((tt, width), F32), pltpu.VMEM((tt, width), F32),
        ],
        compiler_params=_cparams(("arbitrary", "arbitrary")),
        name="rwkv",
    )(p, row(mu), row(w0), row(a0), wwa, g2.astype(BF16), row(k_k), row(k_a), row(r_k),
      row(lnx_g), row(lnx_b), seg, tri, blk)


def _diff_kernel(lq1_ref, lk1_ref, lq2_ref, lk2_ref, sg_ref, q_ref, k_ref, vt_ref, o_ref,
                 m0_ref, a0_ref, m1_ref, a1_ref, sa0_ref, sa1_ref, sb0_ref, sb1_ref,
                 ma0_ref, ma1_ref, mb0_ref, mb1_ref):
    i = pl.program_id(2)
    tq = q_ref.shape[0]
    tk = TK_ATT
    d = DIFF_HEAD
    hw = 2 * d
    nchunk = tk // CHUNK
    neg = -1e30
    qf = q_ref[...].astype(F32) * (d ** -0.5 * math.log2(math.e))
    lane = lax.broadcasted_iota(jnp.int32, qf.shape, 1)
    rchunk = lax.broadcasted_iota(jnp.int32, qf.shape, 0) // CHUNK
    q0 = jnp.where(lane < d, qf, 0.0).astype(BF16)
    q1 = jnp.where(lane >= d, qf, 0.0).astype(BF16)
    c0 = lane - d
    c1 = lane
    in0 = c0 * (nchunk - 1 - c0) >= 0
    in1 = c1 < nchunk
    qm0 = jnp.where(in0, jnp.where(rchunk < c0, neg, 0.0), 0.0).astype(BF16)
    qm1 = jnp.where(in1, jnp.where(rchunk < c1, neg, 0.0), 0.0).astype(BF16)
    keep0 = jnp.where(in0, 0.0, 1.0).astype(BF16)
    keep1 = jnp.where(in1, 0.0, 1.0).astype(BF16)
    hot0 = jnp.where(in0, jnp.where(rchunk == c0, 1.0, 0.0), 0.0).astype(BF16)
    hot1 = jnp.where(in1, jnp.where(rchunk == c1, 1.0, 0.0), 0.0).astype(BF16)

    for m_ref, a_ref in ((m0_ref, a0_ref), (m1_ref, a1_ref)):
        m_ref[...] = jnp.full(m_ref.shape, neg, F32)
        a_ref[...] = jnp.zeros(a_ref.shape, F32)

    def produce(bufs, j):
        s0_ref, s1_ref, x0_ref, x1_ref = bufs
        diag = jnp.where(j == i, 1.0, 0.0).astype(BF16)
        kb = k_ref[pl.ds(pl.multiple_of(j * tk, tk), tk), :]
        s0 = _dot_nt(kb * keep0 + hot0, q0 + qm0 * diag)
        s1 = _dot_nt(kb * keep1 + hot1, q1 + qm1 * diag)
        s0_ref[...] = s0
        s1_ref[...] = s1
        x0_ref[...] = jnp.max(s0, axis=0, keepdims=True)
        x1_ref[...] = jnp.max(s1, axis=0, keepdims=True)

    def update(st, smax, vt, m_ref, a_ref):
        m_old = m_ref[...]
        m_new = jnp.maximum(m_old, smax)
        alpha = jnp.exp2(m_old - m_new)
        pt = jnp.exp2(st - m_new).astype(BF16)
        a_ref[...] = alpha * a_ref[...] + _dot(vt, pt)
        m_ref[...] = m_new

    def consume(bufs, j):
        s0_ref, s1_ref, x0_ref, x1_ref = bufs
        vt = vt_ref[:, pl.ds(pl.multiple_of(j * tk, tk), tk)]
        update(s0_ref[...], x0_ref[...], vt, m0_ref, a0_ref)
        update(s1_ref[...], x1_ref[...], vt, m1_ref, a1_ref)

    buf_a = (sa0_ref, sa1_ref, ma0_ref, ma1_ref)
    buf_b = (sb0_ref, sb1_ref, mb0_ref, mb1_ref)
    nb = i + 1
    npairs = (nb - 1) // 2
    produce(buf_a, 0)

    def body(t, carry):
        j = 2 * t
        produce(buf_b, j + 1)
        consume(buf_a, j)
        produce(buf_a, j + 2)
        consume(buf_b, j + 1)
        return carry

    lax.fori_loop(0, npairs, body, 0)
    last = 2 * npairs

    @pl.when(last == i)
    def _():
        consume(buf_a, last)

    @pl.when(last < i)
    def _():
        produce(buf_b, last + 1)
        consume(buf_a, last)
        consume(buf_b, last + 1)

    lam = (jnp.exp(jnp.sum(lq1_ref[...] * lk1_ref[...], axis=-1, keepdims=True))
           - jnp.exp(jnp.sum(lq2_ref[...] * lk2_ref[...], axis=-1, keepdims=True)) + LAMBDA_INIT)
    ot = (a0_ref[:hw, :] / a0_ref[hw:hw + 1, :] - lam * (a1_ref[:hw, :] / a1_ref[hw:hw + 1, :]))
    o = _rms(ot.T, sg_ref[...]) * (1.0 - LAMBDA_INIT)
    o_ref[...] = o.astype(o_ref.dtype)


def _diffattn(qk, vt, batch, seq, lq1, lk1, lq2, lk2, subln_g):
    n = qk.shape[0]
    width = qk.shape[1] // 2
    hw = 2 * DIFF_HEAD
    heads = width // hw
    tq = TQ_ATT
    assert TK_ATT == tq and seq % tq == 0
    nq = seq // tq
    vec = lambda w: pl.BlockSpec((1, w), lambda b, h, i: (0, 0))
    row = lambda a: a.reshape(1, -1)
    stat = pltpu.VMEM((1, tq), F32)
    acc = pltpu.VMEM((VT_ROWS, tq), F32)
    return pl.pallas_call(
        _diff_kernel,
        grid=(batch, heads, nq),
        in_specs=[
            vec(DIFF_HEAD), vec(DIFF_HEAD), vec(DIFF_HEAD), vec(DIFF_HEAD), vec(hw),
            pl.BlockSpec((tq, hw), lambda b, h, i: (b * nq + i, h)),
            pl.BlockSpec((seq, hw), lambda b, h, i: (b, heads + h)),
            pl.BlockSpec((VT_ROWS, seq), lambda b, h, i: (h, b)),
        ],
        out_specs=pl.BlockSpec((tq, hw), lambda b, h, i: (b * nq + i, h)),
        out_shape=jax.ShapeDtypeStruct((n, width), BF16),
        scratch_shapes=[stat, acc, stat, acc] + [pltpu.VMEM((TK_ATT, tq), F32)] * 4 + [stat] * 4,
        compiler_params=_cparams(("arbitrary", "arbitrary", "arbitrary")),
        name="diffattn",
    )(row(lq1), row(lk1), row(lq2), row(lk2), row(subln_g), qk, qk, vt)


def _memkv_kernel(m_ref, g_ref, w_ref, o_ref):
    o_ref[...] = _dot(_rms(m_ref[...], g_ref[...]).astype(BF16), w_ref[...]).astype(o_ref.dtype)


def _memkv(mem2, g, wkv):
    n, d = mem2.shape
    c = wkv.shape[1]
    tn = 512
    return pl.pallas_call(
        _memkv_kernel,
        grid=(c // tn,),
        in_specs=[
            pl.BlockSpec((n, d), lambda j: (0, 0)),
            pl.BlockSpec((1, d), lambda j: (0, 0)),
            pl.BlockSpec((d, tn), lambda j: (0, j)),
        ],
        out_specs=pl.BlockSpec((n, tn), lambda j: (0, j)),
        out_shape=jax.ShapeDtypeStruct((n, c), BF16),
        compiler_params=_cparams(("arbitrary",)),
        name="memkv",
    )(mem2, g.reshape(1, d), wkv.astype(BF16))


def _mid_kernel(x_ref, yr_ref, yd_ref, wo1_ref, wo2_ref, g2_ref, wq_ref, km_ref, vm_ref, wox_ref,
                g3_ref, rw_ref, rb_ref, tril_ref,
                h_ref, xn_ref, ti_ref, gt_ref, rk_ref, cnt_ref, base_ref):
    step = pl.program_id(0) * pl.num_programs(1) + pl.program_id(1)
    tm, dm = x_ref.shape
    hd = dm // XATTN_HEADS

    @pl.when(step == 0)
    def _():
        base_ref[...] = jnp.zeros(base_ref.shape, F32)

    h1 = x_ref[...] + _dot(yr_ref[...], wo1_ref[...]) + _dot(yd_ref[...], wo2_ref[...])
    hn = _rms(h1, g2_ref[...]).astype(BF16)
    q = (_dot(hn, wq_ref[...]) * (hd ** -0.5)).astype(BF16)
    outs = []
    for hh in range(XATTN_HEADS):
        sl = slice(hh * hd, (hh + 1) * hd)
        s = _dot_nt(q[:, sl], km_ref[:, sl])
        s = s - jnp.max(s, axis=-1, keepdims=True)
        e = jnp.exp(s)
        pr = e / jnp.sum(e, axis=-1, keepdims=True)
        outs.append(_dot(pr.astype(BF16), vm_ref[:, sl]))
    o = jnp.concatenate(outs, axis=-1).astype(BF16)
    h2 = h1 + _dot(o, wox_ref[...])
    h_ref[...] = h2
    xn = _rms(h2, g3_ref[...])
    _store_rows(xn_ref, xn)

    x_hi, x_lo = _split2(xn)
    logits = _dot(x_hi, rw_ref[0]) + (_dot(x_hi, rw_ref[1]) + _dot(x_lo, rw_ref[0])) + rb_ref[...]

    ne = logits.shape[1]
    eio = lax.broadcasted_iota(jnp.int32, (tm, ne), 1).astype(F32)
    work = logits
    vals, idxs, sels = [], [], []
    for _ in range(TOP_K):
        mx = jnp.max(work, axis=-1, keepdims=True)
        ix = jnp.min(jnp.where(work == mx, eio, float(ne)), axis=-1, keepdims=True)
        sel = eio == ix
        vals.append(mx)
        idxs.append(ix)
        sels.append(sel)
        work = jnp.where(sel, -jnp.inf, work)
    es = [jnp.exp(vv - vals[0]) for vv in vals]
    den = es[0] + es[1] + es[2] + es[3]
    gates = [ee / den for ee in es]

    assign = jnp.zeros((tm, ne), F32)
    for sel in sels:
        assign = assign + sel.astype(F32)
    before = _dot(tril_ref[...], assign.astype(BF16)) + base_ref[...]
    ranks = [jnp.sum(jnp.where(sel, before, 0.0), axis=-1, keepdims=True) for sel in sels]
    base_new = base_ref[...] + jnp.sum(assign, axis=0, keepdims=True)
    base_ref[...] = base_new
    cnt_ref[...] = base_new.astype(jnp.int32)

    kio = lax.broadcasted_iota(jnp.int32, (tm, TOP_K), 1)

    def cols(parts):
        out = jnp.where(kio == 0, parts[0], parts[1])
        for kk in range(2, TOP_K):
            out = jnp.where(kio == kk, parts[kk], out)
        return out

    ti_ref[...] = cols(idxs).astype(jnp.int32)
    gt_ref[...] = cols(gates)
    rk_ref[...] = cols(ranks).astype(jnp.int32)


def _mid(x2, yr, yd, batch, seq, w_out, norm2_g, wq, kv, wo, norm3_g, router_w, router_b):
    n, d = x2.shape
    tm = TM_MID
    nt = seq // tm
    mem_len = kv.shape[0] // batch
    wr = yr.shape[1]
    wo1 = w_out[:wr].astype(BF16)
    wo2 = w_out[wr:].astype(BF16)
    rw = jnp.stack(_split2(router_w))
    ne = router_w.shape[1]
    tril = (jnp.arange(tm)[None, :] < jnp.arange(tm)[:, None]).astype(BF16)
    tok = lambda w: pl.BlockSpec((tm, w), lambda b, t: (b * nt + t, 0))
    cst = lambda a: pl.BlockSpec(a.shape, lambda b, t: (0,) * a.ndim)
    row = lambda a: a.reshape(1, -1)
    return pl.pallas_call(
        _mid_kernel,
        grid=(batch, nt),
        in_specs=[
            tok(d), tok(wr), tok(d - wr), cst(wo1), cst(wo2), cst(row(norm2_g)), cst(wq),
            pl.BlockSpec((mem_len, d), lambda b, t: (b, 0)),
            pl.BlockSpec((mem_len, d), lambda b, t: (b, 1)),
            cst(wo), cst(row(norm3_g)), cst(rw), cst(row(router_b)), cst(tril),
        ],
        out_specs=[tok(d), pl.BlockSpec((tm * ROW_TILE, LANES), lambda b, t: (b * nt + t, 0)),
                   tok(TOP_K), tok(TOP_K), tok(TOP_K),
                   pl.BlockSpec((1, ne), lambda b, t: (0, 0))],
        out_shape=[
            jax.ShapeDtypeStruct((n, d), F32),
            jax.ShapeDtypeStruct((n * ROW_TILE, LANES), F32),
            jax.ShapeDtypeStruct((n, TOP_K), jnp.int32),
            jax.ShapeDtypeStruct((n, TOP_K), F32),
            jax.ShapeDtypeStruct((n, TOP_K), jnp.int32),
            jax.ShapeDtypeStruct((1, ne), jnp.int32),
        ],
        scratch_shapes=[pltpu.VMEM((1, ne), F32)],
        compiler_params=_cparams(("arbitrary", "arbitrary")),
        name="mid",
    )(x2, yr, yd, wo1, wo2, row(norm2_g), wq.astype(BF16), kv, kv, wo.astype(BF16),
      row(norm3_g), rw, row(router_b), tril)


def _dispatch_kernel(dest_ref, pad_start_ref, pad_len_ref, x_ref, o_ref, zero_ref, sem, zsem):
    i = pl.program_id(0)
    tm = x_ref.shape[0] // ROW_TILE

    @pl.when(i == 0)
    def _():
        zero_ref[...] = jnp.zeros(zero_ref.shape, zero_ref.dtype)
        pieces = [MOE_TILE >> (s + 1) for s in range(MOE_TILE.bit_length() - 1)]

        def zcopy(nrows, dst_row):
            return pltpu.make_async_copy(
                zero_ref.at[pl.ds(0, nrows * ROW_TILE), :],
                o_ref.at[pl.ds(pl.multiple_of(dst_row * ROW_TILE, ROW_TILE), nrows * ROW_TILE), :], zsem)

        def per_expert(e, carry, wait):
            row = pad_start_ref[e]
            left = pad_len_ref[e]
            for piece in pieces:
                hit = (left & piece) != 0

                @pl.when(hit)
                def _():
                    if wait:
                        zcopy(piece, row).wait()
                    else:
                        zcopy(piece, row).start()

                row = row + jnp.where(hit, piece, 0)
            return carry

        half = MOE_TILE // 2
        tail_start = pad_start_ref[N_EXPERTS]
        tail_pieces = pad_len_ref[N_EXPERTS] // half

        def tail_start_copy(t, carry):
            zcopy(half, tail_start + t * half).start()
            return carry

        def tail_wait_copy(t, carry):
            zcopy(half, tail_start).wait()
            return carry

        lax.fori_loop(0, N_EXPERTS, functools.partial(per_expert, wait=False), 0)
        lax.fori_loop(0, tail_pieces, tail_start_copy, 0)
        lax.fori_loop(0, N_EXPERTS, functools.partial(per_expert, wait=True), 0)
        lax.fori_loop(0, tail_pieces, tail_wait_copy, 0)

    def copy(r8, dst8):
        return pltpu.make_async_copy(x_ref.at[pl.ds(pl.multiple_of(r8, ROW_TILE), ROW_TILE), :],
                                     o_ref.at[pl.ds(pl.multiple_of(dst8, ROW_TILE), ROW_TILE), :], sem)

    def start(r, carry):
        for kk in range(TOP_K):
            copy(r * ROW_TILE, dest_ref[(i * tm + r) * TOP_K + kk]).start(priority=kk % 2)
        return carry

    lax.fori_loop(0, tm, start, 0, unroll=DMA_UNROLL)

    def wait(r, carry):
        for kk in range(TOP_K):
            copy(0, 0).wait()
        return carry

    lax.fori_loop(0, tm, wait, 0, unroll=WAIT_UNROLL)


def _dispatch(dest8, pad_start, pad_len, xn, rows_padded):
    tm = TM_DISP
    n = xn.shape[0] // ROW_TILE
    return pl.pallas_call(
        _dispatch_kernel,
        grid_spec=pltpu.PrefetchScalarGridSpec(
            num_scalar_prefetch=3,
            grid=(n // tm,),
            in_specs=[pl.BlockSpec((tm * ROW_TILE, LANES), lambda i, *_: (i, 0))],
            out_specs=pl.BlockSpec(memory_space=pl.ANY),
            scratch_shapes=[pltpu.VMEM((MOE_TILE // 2 * ROW_TILE, LANES), xn.dtype),
                            pltpu.SemaphoreType.DMA(()), pltpu.SemaphoreType.DMA(())],
        ),
        out_shape=jax.ShapeDtypeStruct((rows_padded * ROW_TILE, LANES), xn.dtype),
        compiler_params=_cparams(("arbitrary",)),
        name="dispatch",
    )(dest8, pad_start, pad_len, xn)


def _experts_kernel(be_ref, nu_ref, x_ref, w1_ref, b1_ref, w2_ref, b2_ref, o_ref,
                    w1b_ref, w2b_ref, prev_ref):
    j = pl.program_id(0)
    e = be_ref[j]
    ff = w2_ref.shape[0]

    @pl.when(j == 0)
    def _():
        prev_ref[0] = -1

    @pl.when(j < nu_ref[0])
    def _():
        @pl.when(e != prev_ref[0])
        def _():
            w1b_ref[...] = w1_ref[...].astype(BF16)
            w2b_ref[...] = w2_ref[...].astype(BF16)
            prev_ref[0] = e

        h = _dot(_load_rows(x_ref, MOE_TILE).astype(BF16), w1b_ref[...]) + b1_ref[...]
        gate = jnp.minimum(h[:, :ff], SWIGLU_LIMIT)
        lin = jnp.clip(h[:, ff:], -SWIGLU_LIMIT, SWIGLU_LIMIT)
        act = gate * jax.nn.sigmoid(SWIGLU_ALPHA * gate) * (lin + 1.0)
        _store_rows(o_ref, _dot(act.astype(BF16), w2b_ref[...]) + b2_ref[...])

    @pl.when(j >= nu_ref[0])
    def _():
        o_ref[...] = jnp.zeros(o_ref.shape, F32)


def _experts(block_e, nused, xin, w1, b1, w2, b2):
    ne, d, f2 = w1.shape
    rows = xin.shape[0] // ROW_TILE
    ff = w2.shape[1]
    nblk = rows // MOE_TILE
    return pl.pallas_call(
        _experts_kernel,
        grid_spec=pltpu.PrefetchScalarGridSpec(
            num_scalar_prefetch=2,
            grid=(nblk,),
            in_specs=[
                pl.BlockSpec((MOE_TILE * ROW_TILE, LANES), lambda j, be, nu: (jnp.minimum(j, nu[0] - 1), 0)),
                pl.BlockSpec((None, d, f2), lambda j, be, nu: (be[j], 0, 0)),
                pl.BlockSpec((None, 1, f2), lambda j, be, nu: (be[j], 0, 0)),
                pl.BlockSpec((None, ff, d), lambda j, be, nu: (be[j], 0, 0)),
                pl.BlockSpec((None, 1, d), lambda j, be, nu: (be[j], 0, 0)),
            ],
            out_specs=pl.BlockSpec((MOE_TILE * ROW_TILE, LANES), lambda j, be, nu: (j, 0)),
            scratch_shapes=[pltpu.VMEM((d, f2), BF16), pltpu.VMEM((ff, d), BF16),
                            pltpu.SMEM((1,), jnp.int32)],
        ),
        out_shape=jax.ShapeDtypeStruct(xin.shape, F32),
        compiler_params=_cparams(("arbitrary",)),
        name="experts",
    )(block_e, nused, xin, w1, b1.reshape(ne, 1, f2), w2, b2.reshape(ne, 1, d))


def _combine_kernel(dest_ref, h_ref, gt_ref, g_ref, yb_ref, o_ref, buf_ref, sem):
    i = pl.program_id(0)
    tm = h_ref.shape[0]
    slot = i % 2

    def copy(src8, dst_slot, kk, r8):
        return pltpu.make_async_copy(
            yb_ref.at[pl.ds(pl.multiple_of(src8, ROW_TILE), ROW_TILE), :],
            buf_ref.at[dst_slot, kk, pl.ds(pl.multiple_of(r8, ROW_TILE), ROW_TILE), :], sem.at[dst_slot])

    def issue(tile, dst_slot):
        def start(r, carry):
            for kk in range(TOP_K):
                copy(dest_ref[(tile * tm + r) * TOP_K + kk], dst_slot, kk, r * ROW_TILE).start(priority=kk % 2)
            return carry

        lax.fori_loop(0, tm, start, 0, unroll=DMA_UNROLL)

    @pl.when(i == 0)
    def _():
        issue(0, 0)

    @pl.when(i + 1 < pl.num_programs(0))
    def _():
        issue(i + 1, 1 - slot)

    def wait(r, carry):
        for kk in range(TOP_K):
            copy(0, slot, kk, 0).wait()
        return carry

    lax.fori_loop(0, tm, wait, 0, unroll=WAIT_UNROLL)

    gt = gt_ref[...]
    rg = COMB_ROWS
    for g0 in range(0, tm, rg):
        rows = slice(g0, g0 + rg)
        parts = []
        for j in range(ROW_TILE):
            y = h_ref[rows, j * LANES:(j + 1) * LANES]
            for kk in range(TOP_K):
                y = y + gt[rows, kk:kk + 1] * buf_ref[slot, kk, pl.ds(g0 * ROW_TILE + j, rg, stride=ROW_TILE), :]
            parts.append(y)
        y = jnp.concatenate(parts, axis=1)
        o_ref[rows, :] = _rms(y, g_ref[...])


def _combine(dest_flat, h2, gates, final_g, yb):
    n, d = h2.shape
    tm = TM_COMB
    return pl.pallas_call(
        _combine_kernel,
        grid_spec=pltpu.PrefetchScalarGridSpec(
            num_scalar_prefetch=1,
            grid=(n // tm,),
            in_specs=[
                pl.BlockSpec((tm, d), lambda i, dest: (i, 0)),
                pl.BlockSpec((tm, TOP_K), lambda i, dest: (i, 0)),
                pl.BlockSpec((1, d), lambda i, dest: (0, 0)),
                pl.BlockSpec(memory_space=pl.ANY),
            ],
            out_specs=pl.BlockSpec((tm, d), lambda i, dest: (i, 0)),
            scratch_shapes=[pltpu.VMEM((2, TOP_K, tm * ROW_TILE, LANES), F32), pltpu.SemaphoreType.DMA((2,))],
        ),
        out_shape=jax.ShapeDtypeStruct((n, d), F32),
        compiler_params=_cparams(("arbitrary",)),
        name="combine",
    )(dest_flat, h2, gates, final_g.reshape(1, d), yb)


def kernel(x, mem, norm1_g, w_in, mu_shift, w0, w2, a0, a2, g2, k_k, k_a, r_k, lnx_g, lnx_b,
           lam_q1, lam_k1, lam_q2, lam_k2, subln_g, w_out, norm2_g, mem_norm_g, wq_x, wkv_x, wo_x,
           norm3_g, router_w, router_b, moe_w1, moe_b1, moe_w2, moe_b2, final_g):
    batch, seq, d = x.shape
    n = batch * seq
    shift_cols = mu_shift.shape[1]
    x2 = x.reshape(n, d)

    p, qk, vt = _inproj(x2, norm1_g[0], w_in[0], shift_cols)
    y_rwkv = _rwkv(p, batch, seq, mu_shift[0], w0[0], w2[0], a0[0], a2[0], g2[0], k_k[0], k_a[0],
                   r_k[0].reshape(-1), lnx_g[0], lnx_b[0])
    y_diff = _diffattn(qk, vt, batch, seq, lam_q1[0], lam_k1[0], lam_q2[0], lam_k2[0], subln_g[0])
    kv = _memkv(mem.reshape(-1, d), mem_norm_g[0], wkv_x[0])

    h2, xn3, top_i, gates, rank, counts = _mid(
        x2, y_rwkv, y_diff, batch, seq, w_out[0], norm2_g[0], wq_x[0], kv, wo_x[0], norm3_g[0],
        router_w[0], router_b[0])

    counts = counts.reshape(-1)
    padded = ((counts + MOE_TILE - 1) // MOE_TILE) * MOE_TILE
    padded_end = jnp.cumsum(padded)
    padded_off = padded_end - padded
    rows_padded = n * TOP_K + N_EXPERTS * MOE_TILE
    nblk = rows_padded // MOE_TILE
    block_start = jnp.arange(nblk, dtype=jnp.int32) * MOE_TILE
    block_e = jnp.minimum(jnp.sum(block_start[:, None] >= padded_end[None, :], axis=1),
                          N_EXPERTS - 1).astype(jnp.int32)
    nused = (padded_end[-1:] // MOE_TILE).astype(jnp.int32)
    expert_hot = top_i[:, :, None] == jnp.arange(N_EXPERTS, dtype=jnp.int32)
    row_off = jnp.sum(jnp.where(expert_hot, padded_off.astype(jnp.int32), 0), axis=-1)
    dest8 = ((row_off + rank) * ROW_TILE).astype(jnp.int32).reshape(-1)
    pad_start = jnp.concatenate([padded_off + counts, padded_end[-1:]]).astype(jnp.int32)
    pad_len = jnp.concatenate([padded - counts, rows_padded - padded_end[-1:]]).astype(jnp.int32)

    xin = _dispatch(dest8, pad_start, pad_len, xn3, rows_padded)
    yb = _experts(block_e, nused, xin, moe_w1[0], moe_b1[0], moe_w2[0], moe_b2[0])
    out = _combine(dest8, h2, gates, final_g, yb)
    return out.reshape(batch, seq, d)
```

```python
import functools
import math

import jax
import jax.numpy as jnp
from jax import lax
from jax.experimental import pallas as pl
from jax.experimental.pallas import tpu as pltpu

F32 = jnp.float32
BF16 = jnp.bfloat16

CHUNK = 64
RWKV_HEAD = 64
DECAY_LORA = 64
AAA_LORA = 64
GATE_LORA = 128
DIFF_HEAD = 64
XATTN_HEADS = 4
N_EXPERTS = 32
TOP_K = 4
SWIGLU_LIMIT = 7.0
SWIGLU_ALPHA = 1.702
NORM_EPS = 1e-5
GN_EPS = 64e-5
LAMBDA_INIT = 0.8 - 0.6 * math.exp(-0.3 * 0)

VMEM_LIMIT = 56 * 1024 * 1024
TM_PROJ = 512
TT_RWKV = 256
PRE_CHUNKS = 4
VT_ROWS = 144
TQ_ATT = 512
TK_ATT = 512
TM_MID = 1024
TM_DISP = 512
TM_COMB = 512
MOE_TILE = 512
DMA_UNROLL = 8
COMB_ROWS = 64
WAIT_UNROLL = 32
ROW_TILE = 8
LANES = 128


def _cparams(sem):
    return pltpu.CompilerParams(dimension_semantics=sem, vmem_limit_bytes=VMEM_LIMIT)


def _dot(a, b):
    return jnp.dot(a, b, preferred_element_type=F32)


def _dot_nt(a, b):
    return lax.dot_general(a, b, (((1,), (1,)), ((), ())), preferred_element_type=F32)


def _dot_tn(a, b):
    return lax.dot_general(a, b, (((0,), (0,)), ((), ())), preferred_element_type=F32)


def _rms(x, g):
    return x * lax.rsqrt(jnp.mean(x * x, axis=-1, keepdims=True) + NORM_EPS) * g


def _load_rows(ref, nrows):
    return jnp.concatenate([ref[pl.ds(j, nrows, stride=ROW_TILE), :] for j in range(ROW_TILE)], axis=1)


def _store_rows(ref, val):
    nrows = val.shape[0]
    for j in range(ROW_TILE):
        ref[pl.ds(j, nrows, stride=ROW_TILE), :] = val[:, j * LANES:(j + 1) * LANES]


def _split2(x):
    hi = x.astype(BF16)
    lo = (x - hi.astype(F32)).astype(BF16)
    return hi, lo


def _split3(x):
    hi = x.astype(BF16)
    r1 = x - hi.astype(F32)
    mid = r1.astype(BF16)
    lo = (r1 - mid.astype(F32)).astype(BF16)
    return hi, mid, lo


def _inproj_kernel(x_ref, g_ref, wa_ref, wb_ref, wvt_ref, p_ref, qk_ref, vt_ref):
    xn = _rms(x_ref[...], g_ref[...]).astype(BF16)
    p_ref[...] = _dot(xn, wa_ref[...])
    qk_ref[...] = _dot(xn, wb_ref[...]).astype(BF16)
    vt = _dot_nt(wvt_ref[...], xn)
    hw = 2 * DIFF_HEAD
    ones = jnp.ones((VT_ROWS - hw, vt.shape[1]), F32)
    for h in range(vt.shape[0] // hw):
        vt_ref[h * VT_ROWS:(h + 1) * VT_ROWS, :] = jnp.concatenate(
            [vt[h * hw:(h + 1) * hw], ones], axis=0).astype(BF16)


def _inproj(x2, g, w_in, shift_cols):
    n, d = x2.shape
    dw = (w_in.shape[1] - shift_cols) // 3
    wa = w_in[:, :shift_cols].astype(BF16)
    wb = w_in[:, shift_cols:shift_cols + 2 * dw].astype(BF16)
    wvt = w_in[:, shift_cols + 2 * dw:].T.astype(BF16)
    tm = TM_PROJ
    return pl.pallas_call(
        _inproj_kernel,
        grid=(n // tm,),
        in_specs=[
            pl.BlockSpec((tm, d), lambda i: (i, 0)),
            pl.BlockSpec((1, d), lambda i: (0, 0)),
            pl.BlockSpec((d, shift_cols), lambda i: (0, 0)),
            pl.BlockSpec((d, 2 * dw), lambda i: (0, 0)),
            pl.BlockSpec((dw, d), lambda i: (0, 0)),
        ],
        out_specs=[
            pl.BlockSpec((tm, shift_cols), lambda i: (i, 0)),
            pl.BlockSpec((tm, 2 * dw), lambda i: (i, 0)),
            pl.BlockSpec((dw // (2 * DIFF_HEAD) * VT_ROWS, tm), lambda i: (0, i)),
        ],
        out_shape=[
            jax.ShapeDtypeStruct((n, shift_cols), F32),
            jax.ShapeDtypeStruct((n, 2 * dw), BF16),
            jax.ShapeDtypeStruct((dw // (2 * DIFF_HEAD) * VT_ROWS, n), BF16),
        ],
        compiler_params=_cparams(("arbitrary",)),
        name="inproj",
    )(x2, g.reshape(1, d), wa, wb, wvt)


def _rwkv_kernel(p_ref, mu_ref, w0_ref, a0_ref, wwa_ref, g2_ref, kk_ref, ka_ref, rk_ref,
                 lg_ref, lb_ref, seg_ref, tri_ref, blk_ref, o_ref,
                 s_ref, last_ref, ar_ref, bk_ref, bkh_ref, zv_ref, gc_ref, ry_ref, mn_ref, y_ref,
                 bonus_ref, gate_ref, *, width, heads):
    t_idx = pl.program_id(1)
    tt = p_ref.shape[0]
    nchunk = tt // CHUNK
    hd = RWKV_HEAD
    hs = range(heads)

    @pl.when(t_idx == 0)
    def _():
        s_ref[...] = jnp.zeros(s_ref.shape, F32)
        last_ref[...] = jnp.zeros(last_ref.shape, F32)
        zv_ref[...] = jnp.zeros(zv_ref.shape, BF16)

    p = p_ref[...]
    prev = pltpu.roll(p, shift=1, axis=0)
    row0 = lax.broadcasted_iota(jnp.int32, (tt, 1), 0) == 0
    prev = jnp.where(row0, last_ref[0:1, :], prev)
    last_ref[0:1, :] = p[tt - 1:tt, :]
    pm = p + (prev - p) * mu_ref[...]

    r = pm[:, 0:width]
    k = pm[:, width:2 * width]
    v = pm[:, 2 * width:3 * width]
    wa = pm[:, 3 * width:3 * width + DECAY_LORA + AAA_LORA]
    gd = pm[:, 3 * width + DECAY_LORA + AAA_LORA:]

    lane = lax.broadcasted_iota(jnp.int32, wa.shape, 1)
    z = jnp.where(lane < DECAY_LORA, jnp.tanh(wa), wa).astype(BF16)
    lora = _dot(z, wwa_ref[...])
    logw = -math.exp(-0.5) * jax.nn.sigmoid(w0_ref[...] + lora[:, :width])
    lr = jax.nn.sigmoid(a0_ref[...] + lora[:, width:])
    gate_ref[...] = _dot(jax.nn.sigmoid(gd).astype(BF16), g2_ref[...])

    seg = seg_ref[...]

    def segsum(x):
        hi, lo = _split2(x)
        return _dot(hi, seg) + _dot(lo, seg)

    kk = k * kk_ref[...]
    kk = kk / jnp.maximum(jnp.sqrt(segsum(kk * kk)), 1e-12)
    k2 = k * (1.0 + (lr - 1.0) * ka_ref[...])
    bonus_ref[...] = segsum(r * k2 * rk_ref[...]) * v

    h3 = _split3(logw)
    tri = tri_ref[...]
    blk = blk_ref[...]
    cum = _dot(tri, h3[0]) + _dot(tri, h3[1]) + _dot(tri, h3[2])
    tot = _dot(blk, h3[0]) + _dot(blk, h3[1]) + _dot(blk, h3[2])
    einv = jnp.exp(-cum)
    ed = jnp.exp(tot - cum)
    b = kk * lr
    parts = (
        (-kk * jnp.exp(cum - logw), ar_ref, 0), (r * jnp.exp(cum), ar_ref, CHUNK),
        (b * einv, bk_ref, 0), (k2 * einv, bk_ref, CHUNK),
        (b * ed, bkh_ref, 0), (k2 * ed, bkh_ref, CHUNK),
    )
    gc = jnp.exp(tot)
    zlane = jnp.zeros((tt, hd), F32)
    for h in hs:
        sl = slice(h * hd, (h + 1) * hd)
        gc_ref[h] = gc[:, sl]
        zv = jnp.concatenate([zlane, v[:, sl]], axis=1).astype(BF16)
        for c in range(nchunk):
            zv_ref[h, c, CHUNK:, :] = zv[c * CHUNK:(c + 1) * CHUNK]
        for val, ref, off in parts:
            vb = val[:, sl].astype(BF16)
            for c in range(nchunk):
                ref[h, c, off:off + CHUNK, :] = vb[c * CHUNK:(c + 1) * CHUNK]

    ri = lax.broadcasted_iota(jnp.int32, (2 * CHUNK, 2 * CHUNK), 0)
    ci = lax.broadcasted_iota(jnp.int32, (2 * CHUNK, 2 * CHUNK), 1)
    mask = ci % CHUNK < ri % CHUNK + ri // CHUNK
    e_r = lax.broadcasted_iota(jnp.int32, (CHUNK, CHUNK), 0)
    e_c = lax.broadcasted_iota(jnp.int32, (CHUNK, CHUNK), 1)
    eye = (e_r == e_c).astype(F32)
    zpad = jnp.zeros((CHUNK, hd), F32)

    def pre_body(cc, carry):
        cs = [cc * PRE_CHUNKS + u for u in range(PRE_CHUNKS)]
        ch = [(c, h) for c in cs for h in hs]
        idx = range(len(ch))
        ar = [ar_ref[h, c] for c, h in ch]
        g = [jnp.where(mask, _dot_nt(ar[n], bk_ref[h, c]), 0.0) for n, (c, h) in enumerate(ch)]
        gb = [x.astype(BF16) for x in g]
        gv = [_dot(gb[n], zv_ref[h, c]) for n, (c, h) in enumerate(ch)]
        a = [x[:CHUNK, :hd] for x in g]
        ab = [x.astype(BF16) for x in a]
        pw = [_dot(x, x) for x in ab]
        xs = [eye + x for x in a]
        for _ in range(4):
            res = [_dot(jnp.concatenate([xs[n], pw[n]], axis=0).astype(BF16), pw[n].astype(BF16))
                   for n in idx]
            xs = [xs[n] + res[n][:CHUNK] for n in idx]
            pw = [res[n][CHUNK:] for n in idx]
        tinv = [xs[n] + _dot(xs[n].astype(BF16), pw[n].astype(BF16)) for n in idx]
        e = [jnp.concatenate([ar[n][:CHUNK].astype(F32), zpad], axis=1) + gv[n][:CHUNK] for n in idx]
        wb = [_dot(tinv[n].astype(BF16), e[n].astype(BF16)).astype(BF16) for n in idx]
        for n, (c, h) in enumerate(ch):
            rows = pl.ds(pl.multiple_of(c * CHUNK, CHUNK), CHUNK)
            ry_ref[h, rows, :] = (jnp.concatenate([ar[n][CHUNK:].astype(F32), zpad], axis=1)
                                  + gv[n][CHUNK:] + _dot(gb[n][CHUNK:, :hd], wb[n]))
            left = jnp.concatenate([wb[n], zv_ref[h, c, CHUNK:, :]], axis=0)
            mn_ref[h, c] = _dot_tn(left, bkh_ref[h, c])
        return carry

    lax.fori_loop(0, nchunk // PRE_CHUNKS, pre_body, 0)

    def seq_body(c, carry):
        r0 = pl.multiple_of(c * CHUNK, CHUNK)
        rows = pl.ds(r0, CHUNK)
        s0 = [s_ref[h] for h in hs]
        s0b = [x.astype(BF16) for x in s0]
        ry = [ry_ref[h, rows, :] for h in hs]
        mn = [mn_ref[h, c] for h in hs]
        for h in hs:
            y_ref[h, rows, :] = ry[h] + _dot_nt(ry[h][:, :hd].astype(BF16), s0b[h])
            s_ref[h, CHUNK:, :] = (s0[h][CHUNK:] * gc_ref[h, pl.ds(r0, 1), :]
                                   + _dot(s0b[h][CHUNK:], mn[h][:CHUNK].astype(BF16)) + mn[h][CHUNK:])
        return carry

    lax.fori_loop(0, nchunk, seq_body, 0)

    y = jnp.concatenate([y_ref[h][:, hd:] for h in hs], axis=-1)
    mean = segsum(y) * (1.0 / hd)
    dlt = y - mean
    var = segsum(dlt * dlt) * (1.0 / hd)
    yn = dlt * lax.rsqrt(var + GN_EPS) * lg_ref[...] + lb_ref[...]
    o_ref[...] = ((yn + bonus_ref[...]) * gate_ref[...]).astype(o_ref.dtype)


def _rwkv(p, batch, seq, mu, w0, w2, a0, a2, g2, k_k, k_a, r_k, lnx_g, lnx_b):
    n, cols = p.shape
    width = w0.shape[0]
    heads = width // RWKV_HEAD
    assert RWKV_HEAD == CHUNK
    tt = TT_RWKV
    nt = seq // tt
    nchunk = tt // CHUNK
    wwa = jnp.zeros((DECAY_LORA + AAA_LORA, 2 * width), F32)
    wwa = wwa.at[:DECAY_LORA, :width].set(w2).at[DECAY_LORA:, width:].set(a2).astype(BF16)
    lane_head = jnp.arange(width) // RWKV_HEAD
    seg = (lane_head[:, None] == lane_head[None, :]).astype(BF16)
    tch = jnp.arange(tt) // CHUNK
    same = tch[:, None] == tch[None, :]
    tri = (same & (jnp.arange(tt)[None, :] <= jnp.arange(tt)[:, None])).astype(BF16)
    blk = same.astype(BF16)
    row = lambda a: a.reshape(1, -1)
    vec = lambda w: pl.BlockSpec((1, w), lambda b, t: (0, 0))
    full2 = lambda a: pl.BlockSpec(a.shape, lambda b, t: (0, 0))
    slab = lambda lanes, dt: pltpu.VMEM((heads, nchunk, 2 * CHUNK, lanes), dt)
    kern = functools.partial(_rwkv_kernel, width=width, heads=heads)
    return pl.pallas_call(
        kern,
        grid=(batch, nt),
        in_specs=[
            pl.BlockSpec((tt, cols), lambda b, t: (b * nt + t, 0)),
            vec(cols), vec(width), vec(width), full2(wwa), pl.BlockSpec(g2.shape, lambda b, t: (0, 0)),
            vec(width), vec(width), vec(width), vec(width), vec(width),
            full2(seg), full2(tri), full2(blk),
        ],
        out_specs=pl.BlockSpec((tt, width), lambda b, t: (b * nt + t, 0)),
        out_shape=jax.ShapeDtypeStruct((n, width), BF16),
        scratch_shapes=[
            pltpu.VMEM((heads, 2 * RWKV_HEAD, RWKV_HEAD), F32),
            pltpu.VMEM((8, cols), F32),
            slab(RWKV_HEAD, BF16),
            slab(RWKV_HEAD, BF16),
            slab(RWKV_HEAD, BF16),
            slab(2 * RWKV_HEAD, BF16),
            pltpu.VMEM((heads, tt, RWKV_HEAD), F32),
            pltpu.VMEM((heads, tt, 2 * RWKV_HEAD), F32),
            slab(RWKV_HEAD, F32),
            pltpu.VMEM((heads, tt, 2 * RWKV_HEAD), F32),
            pltpu.VMEM((tt, width), F32), pltpu.VMEM((tt, width), F32),
        ],
        compiler_params=_cparams(("arbitrary", "arbitrary")),
        name="rwkv",
    )(p, row(mu), row(w0), row(a0), wwa, g2.astype(BF16), row(k_k), row(k_a), row(r_k),
      row(lnx_g), row(lnx_b), seg, tri, blk)


def _diff_kernel(lq1_ref, lk1_ref, lq2_ref, lk2_ref, sg_ref, q_ref, k_ref, vt_ref, o_ref,
                 m0_ref, a0_ref, m1_ref, a1_ref, sa0_ref, sa1_ref, sb0_ref, sb1_ref,
                 ma0_ref, ma1_ref, mb0_ref, mb1_ref):
    i = pl.program_id(2)
    tq = q_ref.shape[0]
    tk = TK_ATT
    d = DIFF_HEAD
    hw = 2 * d
    nchunk = tk // CHUNK
    neg = -1e30
    qf = q_ref[...].astype(F32) * (d ** -0.5 * math.log2(math.e))
    lane = lax.broadcasted_iota(jnp.int32, qf.shape, 1)
    rchunk = lax.broadcasted_iota(jnp.int32, qf.shape, 0) // CHUNK
    q0 = jnp.where(lane < d, qf, 0.0).astype(BF16)
    q1 = jnp.where(lane >= d, qf, 0.0).astype(BF16)
    c0 = lane - d
    c1 = lane
    in0 = c0 * (nchunk - 1 - c0) >= 0
    in1 = c1 < nchunk
    qm0 = jnp.where(in0, jnp.where(rchunk < c0, neg, 0.0), 0.0).astype(BF16)
    qm1 = jnp.where(in1, jnp.where(rchunk < c1, neg, 0.0), 0.0).astype(BF16)
    keep0 = jnp.where(in0, 0.0, 1.0).astype(BF16)
    keep1 = jnp.where(in1, 0.0, 1.0).astype(BF16)
    hot0 = jnp.where(in0, jnp.where(rchunk == c0, 1.0, 0.0), 0.0).astype(BF16)
    hot1 = jnp.where(in1, jnp.where(rchunk == c1, 1.0, 0.0), 0.0).astype(BF16)

    for m_ref, a_ref in ((m0_ref, a0_ref), (m1_ref, a1_ref)):
        m_ref[...] = jnp.full(m_ref.shape, neg, F32)
        a_ref[...] = jnp.zeros(a_ref.shape, F32)

    def produce(bufs, j):
        s0_ref, s1_ref, x0_ref, x1_ref = bufs
        diag = jnp.where(j == i, 1.0, 0.0).astype(BF16)
        kb = k_ref[pl.ds(pl.multiple_of(j * tk, tk), tk), :]
        s0 = _dot_nt(kb * keep0 + hot0, q0 + qm0 * diag)
        s1 = _dot_nt(kb * keep1 + hot1, q1 + qm1 * diag)
        s0_ref[...] = s0
        s1_ref[...] = s1
        x0_ref[...] = jnp.max(s0, axis=0, keepdims=True)
        x1_ref[...] = jnp.max(s1, axis=0, keepdims=True)

    def update(st, smax, vt, m_ref, a_ref):
        m_old = m_ref[...]
        m_new = jnp.maximum(m_old, smax)
        alpha = jnp.exp2(m_old - m_new)
        pt = jnp.exp2(st - m_new).astype(BF16)
        a_ref[...] = alpha * a_ref[...] + _dot(vt, pt)
        m_ref[...] = m_new

    def consume(bufs, j):
        s0_ref, s1_ref, x0_ref, x1_ref = bufs
        vt = vt_ref[:, pl.ds(pl.multiple_of(j * tk, tk), tk)]
        update(s0_ref[...], x0_ref[...], vt, m0_ref, a0_ref)
        update(s1_ref[...], x1_ref[...], vt, m1_ref, a1_ref)

    buf_a = (sa0_ref, sa1_ref, ma0_ref, ma1_ref)
    buf_b = (sb0_ref, sb1_ref, mb0_ref, mb1_ref)
    nb = i + 1
    npairs = (nb - 1) // 2
    produce(buf_a, 0)

    def body(t, carry):
        j = 2 * t
        produce(buf_b, j + 1)
        consume(buf_a, j)
        produce(buf_a, j + 2)
        consume(buf_b, j + 1)
        return carry

    lax.fori_loop(0, npairs, body, 0)
    last = 2 * npairs

    @pl.when(last == i)
    def _():
        consume(buf_a, last)

    @pl.when(last < i)
    def _():
        produce(buf_b, last + 1)
        consume(buf_a, last)
        consume(buf_b, last + 1)

    lam = (jnp.exp(jnp.sum(lq1_ref[...] * lk1_ref[...], axis=-1, keepdims=True))
           - jnp.exp(jnp.sum(lq2_ref[...] * lk2_ref[...], axis=-1, keepdims=True)) + LAMBDA_INIT)
    ot = (a0_ref[:hw, :] / a0_ref[hw:hw + 1, :] - lam * (a1_ref[:hw, :] / a1_ref[hw:hw + 1, :]))
    o = _rms(ot.T, sg_ref[...]) * (1.0 - LAMBDA_INIT)
    o_ref[...] = o.astype(o_ref.dtype)


def _diffattn(qk, vt, batch, seq, lq1, lk1, lq2, lk2, subln_g):
    n = qk.shape[0]
    width = qk.shape[1] // 2
    hw = 2 * DIFF_HEAD
    heads = width // hw
    tq = TQ_ATT
    assert TK_ATT == tq and seq % tq == 0
    nq = seq // tq
    vec = lambda w: pl.BlockSpec((1, w), lambda b, h, i: (0, 0))
    row = lambda a: a.reshape(1, -1)
    stat = pltpu.VMEM((1, tq), F32)
    acc = pltpu.VMEM((VT_ROWS, tq), F32)
    return pl.pallas_call(
        _diff_kernel,
        grid=(batch, heads, nq),
        in_specs=[
            vec(DIFF_HEAD), vec(DIFF_HEAD), vec(DIFF_HEAD), vec(DIFF_HEAD), vec(hw),
            pl.BlockSpec((tq, hw), lambda b, h, i: (b * nq + i, h)),
            pl.BlockSpec((seq, hw), lambda b, h, i: (b, heads + h)),
            pl.BlockSpec((VT_ROWS, seq), lambda b, h, i: (h, b)),
        ],
        out_specs=pl.BlockSpec((tq, hw), lambda b, h, i: (b * nq + i, h)),
        out_shape=jax.ShapeDtypeStruct((n, width), BF16),
        scratch_shapes=[stat, acc, stat, acc] + [pltpu.VMEM((TK_ATT, tq), F32)] * 4 + [stat] * 4,
        compiler_params=_cparams(("arbitrary", "arbitrary", "arbitrary")),
        name="diffattn",
    )(row(lq1), row(lk1), row(lq2), row(lk2), row(subln_g), qk, qk, vt)


def _memkv_kernel(m_ref, g_ref, w_ref, o_ref):
    o_ref[...] = _dot(_rms(m_ref[...], g_ref[...]).astype(BF16), w_ref[...]).astype(o_ref.dtype)


def _memkv(mem2, g, wkv):
    n, d = mem2.shape
    c = wkv.shape[1]
    tn = 512
    return pl.pallas_call(
        _memkv_kernel,
        grid=(c // tn,),
        in_specs=[
            pl.BlockSpec((n, d), lambda j: (0, 0)),
            pl.BlockSpec((1, d), lambda j: (0, 0)),
            pl.BlockSpec((d, tn), lambda j: (0, j)),
        ],
        out_specs=pl.BlockSpec((n, tn), lambda j: (0, j)),
        out_shape=jax.ShapeDtypeStruct((n, c), BF16),
        compiler_params=_cparams(("arbitrary",)),
        name="memkv",
    )(mem2, g.reshape(1, d), wkv.astype(BF16))


def _mid_kernel(x_ref, yr_ref, yd_ref, wo1_ref, wo2_ref, g2_ref, wq_ref, km_ref, vm_ref, wox_ref,
                g3_ref, rw_ref, rb_ref, tril_ref,
                h_ref, xn_ref, ti_ref, gt_ref, rk_ref, cnt_ref, base_ref):
    step = pl.program_id(0) * pl.num_programs(1) + pl.program_id(1)
    tm, dm = x_ref.shape
    hd = dm // XATTN_HEADS

    @pl.when(step == 0)
    def _():
        base_ref[...] = jnp.zeros(base_ref.shape, F32)

    h1 = x_ref[...] + _dot(yr_ref[...], wo1_ref[...]) + _dot(yd_ref[...], wo2_ref[...])
    hn = _rms(h1, g2_ref[...]).astype(BF16)
    q = (_dot(hn, wq_ref[...]) * (hd ** -0.5)).astype(BF16)
    outs = []
    for hh in range(XATTN_HEADS):
        sl = slice(hh * hd, (hh + 1) * hd)
        s = _dot_nt(q[:, sl], km_ref[:, sl])
        s = s - jnp.max(s, axis=-1, keepdims=True)
        e = jnp.exp(s)
        pr = e / jnp.sum(e, axis=-1, keepdims=True)
        outs.append(_dot(pr.astype(BF16), vm_ref[:, sl]))
    o = jnp.concatenate(outs, axis=-1).astype(BF16)
    h2 = h1 + _dot(o, wox_ref[...])
    h_ref[...] = h2
    xn = _rms(h2, g3_ref[...])
    _store_rows(xn_ref, xn)

    x_hi, x_lo = _split2(xn)
    logits = _dot(x_hi, rw_ref[0]) + (_dot(x_hi, rw_ref[1]) + _dot(x_lo, rw_ref[0])) + rb_ref[...]

    ne = logits.shape[1]
    eio = lax.broadcasted_iota(jnp.int32, (tm, ne), 1).astype(F32)
    work = logits
    vals, idxs, sels = [], [], []
    for _ in range(TOP_K):
        mx = jnp.max(work, axis=-1, keepdims=True)
        ix = jnp.min(jnp.where(work == mx, eio, float(ne)), axis=-1, keepdims=True)
        sel = eio == ix
        vals.append(mx)
        idxs.append(ix)
        sels.append(sel)
        work = jnp.where(sel, -jnp.inf, work)
    es = [jnp.exp(vv - vals[0]) for vv in vals]
    den = es[0] + es[1] + es[2] + es[3]
    gates = [ee / den for ee in es]

    assign = jnp.zeros((tm, ne), F32)
    for sel in sels:
        assign = assign + sel.astype(F32)
    before = _dot(tril_ref[...], assign.astype(BF16)) + base_ref[...]
    ranks = [jnp.sum(jnp.where(sel, before, 0.0), axis=-1, keepdims=True) for sel in sels]
    base_new = base_ref[...] + jnp.sum(assign, axis=0, keepdims=True)
    base_ref[...] = base_new
    cnt_ref[...] = base_new.astype(jnp.int32)

    kio = lax.broadcasted_iota(jnp.int32, (tm, TOP_K), 1)

    def cols(parts):
        out = jnp.where(kio == 0, parts[0], parts[1])
        for kk in range(2, TOP_K):
            out = jnp.where(kio == kk, parts[kk], out)
        return out

    ti_ref[...] = cols(idxs).astype(jnp.int32)
    gt_ref[...] = cols(gates)
    rk_ref[...] = cols(ranks).astype(jnp.int32)


def _mid(x2, yr, yd, batch, seq, w_out, norm2_g, wq, kv, wo, norm3_g, router_w, router_b):
    n, d = x2.shape
    tm = TM_MID
    nt = seq // tm
    mem_len = kv.shape[0] // batch
    wr = yr.shape[1]
    wo1 = w_out[:wr].astype(BF16)
    wo2 = w_out[wr:].astype(BF16)
    rw = jnp.stack(_split2(router_w))
    ne = router_w.shape[1]
    tril = (jnp.arange(tm)[None, :] < jnp.arange(tm)[:, None]).astype(BF16)
    tok = lambda w: pl.BlockSpec((tm, w), lambda b, t: (b * nt + t, 0))
    cst = lambda a: pl.BlockSpec(a.shape, lambda b, t: (0,) * a.ndim)
    row = lambda a: a.reshape(1, -1)
    return pl.pallas_call(
        _mid_kernel,
        grid=(batch, nt),
        in_specs=[
            tok(d), tok(wr), tok(d - wr), cst(wo1), cst(wo2), cst(row(norm2_g)), cst(wq),
            pl.BlockSpec((mem_len, d), lambda b, t: (b, 0)),
            pl.BlockSpec((mem_len, d), lambda b, t: (b, 1)),
            cst(wo), cst(row(norm3_g)), cst(rw), cst(row(router_b)), cst(tril),
        ],
        out_specs=[tok(d), pl.BlockSpec((tm * ROW_TILE, LANES), lambda b, t: (b * nt + t, 0)),
                   tok(TOP_K), tok(TOP_K), tok(TOP_K),
                   pl.BlockSpec((1, ne), lambda b, t: (0, 0))],
        out_shape=[
            jax.ShapeDtypeStruct((n, d), F32),
            jax.ShapeDtypeStruct((n * ROW_TILE, LANES), F32),
            jax.ShapeDtypeStruct((n, TOP_K), jnp.int32),
            jax.ShapeDtypeStruct((n, TOP_K), F32),
            jax.ShapeDtypeStruct((n, TOP_K), jnp.int32),
            jax.ShapeDtypeStruct((1, ne), jnp.int32),
        ],
        scratch_shapes=[pltpu.VMEM((1, ne), F32)],
        compiler_params=_cparams(("arbitrary", "arbitrary")),
        name="mid",
    )(x2, yr, yd, wo1, wo2, row(norm2_g), wq.astype(BF16), kv, kv, wo.astype(BF16),
      row(norm3_g), rw, row(router_b), tril)


def _dispatch_kernel(dest_ref, pad_start_ref, pad_len_ref, x_ref, o_ref, zero_ref, sem, zsem):
    i = pl.program_id(0)
    tm = x_ref.shape[0] // ROW_TILE

    @pl.when(i == 0)
    def _():
        zero_ref[...] = jnp.zeros(zero_ref.shape, zero_ref.dtype)
        pieces = [MOE_TILE >> (s + 1) for s in range(MOE_TILE.bit_length() - 1)]

        def zcopy(nrows, dst_row):
            return pltpu.make_async_copy(
                zero_ref.at[pl.ds(0, nrows * ROW_TILE), :],
                o_ref.at[pl.ds(pl.multiple_of(dst_row * ROW_TILE, ROW_TILE), nrows * ROW_TILE), :], zsem)

        def per_expert(e, carry, wait):
            row = pad_start_ref[e]
            left = pad_len_ref[e]
            for piece in pieces:
                hit = (left & piece) != 0

                @pl.when(hit)
                def _():
                    if wait:
                        zcopy(piece, row).wait()
                    else:
                        zcopy(piece, row).start()

                row = row + jnp.where(hit, piece, 0)
            return carry

        half = MOE_TILE // 2
        tail_start = pad_start_ref[N_EXPERTS]
        tail_pieces = pad_len_ref[N_EXPERTS] // half

        def tail_start_copy(t, carry):
            zcopy(half, tail_start + t * half).start()
            return carry

        def tail_wait_copy(t, carry):
            zcopy(half, tail_start).wait()
            return carry

        lax.fori_loop(0, N_EXPERTS, functools.partial(per_expert, wait=False), 0)
        lax.fori_loop(0, tail_pieces, tail_start_copy, 0)
        lax.fori_loop(0, N_EXPERTS, functools.partial(per_expert, wait=True), 0)
        lax.fori_loop(0, tail_pieces, tail_wait_copy, 0)

    def copy(r8, dst8):
        return pltpu.make_async_copy(x_ref.at[pl.ds(pl.multiple_of(r8, ROW_TILE), ROW_TILE), :],
                                     o_ref.at[pl.ds(pl.multiple_of(dst8, ROW_TILE), ROW_TILE), :], sem)

    def start(r, carry):
        for kk in range(TOP_K):
            copy(r * ROW_TILE, dest_ref[(i * tm + r) * TOP_K + kk]).start(priority=kk % 2)
        return carry

    lax.fori_loop(0, tm, start, 0, unroll=DMA_UNROLL)

    def wait(r, carry):
        for kk in range(TOP_K):
            copy(0, 0).wait()
        return carry

    lax.fori_loop(0, tm, wait, 0, unroll=WAIT_UNROLL)


def _dispatch(dest8, pad_start, pad_len, xn, rows_padded):
    tm = TM_DISP
    n = xn.shape[0] // ROW_TILE
    return pl.pallas_call(
        _dispatch_kernel,
        grid_spec=pltpu.PrefetchScalarGridSpec(
            num_scalar_prefetch=3,
            grid=(n // tm,),
            in_specs=[pl.BlockSpec((tm * ROW_TILE, LANES), lambda i, *_: (i, 0))],
            out_specs=pl.BlockSpec(memory_space=pl.ANY),
            scratch_shapes=[pltpu.VMEM((MOE_TILE // 2 * ROW_TILE, LANES), xn.dtype),
                            pltpu.SemaphoreType.DMA(()), pltpu.SemaphoreType.DMA(())],
        ),
        out_shape=jax.ShapeDtypeStruct((rows_padded * ROW_TILE, LANES), xn.dtype),
        compiler_params=_cparams(("arbitrary",)),
        name="dispatch",
    )(dest8, pad_start, pad_len, xn)


def _experts_kernel(be_ref, nu_ref, x_ref, w1_ref, b1_ref, w2_ref, b2_ref, o_ref,
                    w1b_ref, w2b_ref, prev_ref):
    j = pl.program_id(0)
    e = be_ref[j]
    ff = w2_ref.shape[0]

    @pl.when(j == 0)
    def _():
        prev_ref[0] = -1

    @pl.when(j < nu_ref[0])
    def _():
        @pl.when(e != prev_ref[0])
        def _():
            w1b_ref[...] = w1_ref[...].astype(BF16)
            w2b_ref[...] = w2_ref[...].astype(BF16)
            prev_ref[0] = e

        h = _dot(_load_rows(x_ref, MOE_TILE).astype(BF16), w1b_ref[...]) + b1_ref[...]
        gate = jnp.minimum(h[:, :ff], SWIGLU_LIMIT)
        lin = jnp.clip(h[:, ff:], -SWIGLU_LIMIT, SWIGLU_LIMIT)
        act = gate * jax.nn.sigmoid(SWIGLU_ALPHA * gate) * (lin + 1.0)
        _store_rows(o_ref, _dot(act.astype(BF16), w2b_ref[...]) + b2_ref[...])

    @pl.when(j >= nu_ref[0])
    def _():
        o_ref[...] = jnp.zeros(o_ref.shape, F32)


def _experts(block_e, nused, xin, w1, b1, w2, b2):
    ne, d, f2 = w1.shape
    rows = xin.shape[0] // ROW_TILE
    ff = w2.shape[1]
    nblk = rows // MOE_TILE
    return pl.pallas_call(
        _experts_kernel,
        grid_spec=pltpu.PrefetchScalarGridSpec(
            num_scalar_prefetch=2,
            grid=(nblk,),
            in_specs=[
                pl.BlockSpec((MOE_TILE * ROW_TILE, LANES), lambda j, be, nu: (jnp.minimum(j, nu[0] - 1), 0)),
                pl.BlockSpec((None, d, f2), lambda j, be, nu: (be[j], 0, 0)),
                pl.BlockSpec((None, 1, f2), lambda j, be, nu: (be[j], 0, 0)),
                pl.BlockSpec((None, ff, d), lambda j, be, nu: (be[j], 0, 0)),
                pl.BlockSpec((None, 1, d), lambda j, be, nu: (be[j], 0, 0)),
            ],
            out_specs=pl.BlockSpec((MOE_TILE * ROW_TILE, LANES), lambda j, be, nu: (j, 0)),
            scratch_shapes=[pltpu.VMEM((d, f2), BF16), pltpu.VMEM((ff, d), BF16),
                            pltpu.SMEM((1,), jnp.int32)],
        ),
        out_shape=jax.ShapeDtypeStruct(xin.shape, F32),
        compiler_params=_cparams(("arbitrary",)),
        name="experts",
    )(block_e, nused, xin, w1, b1.reshape(ne, 1, f2), w2, b2.reshape(ne, 1, d))


def _combine_kernel(dest_ref, h_ref, gt_ref, g_ref, yb_ref, o_ref, buf_ref, sem):
    i = pl.program_id(0)
    tm = h_ref.shape[0]
    slot = i % 2

    def copy(src8, dst_slot, kk, r8):
        return pltpu.make_async_copy(
            yb_ref.at[pl.ds(pl.multiple_of(src8, ROW_TILE), ROW_TILE), :],
            buf_ref.at[dst_slot, kk, pl.ds(pl.multiple_of(r8, ROW_TILE), ROW_TILE), :], sem.at[dst_slot])

    def issue(tile, dst_slot):
        def start(r, carry):
            for kk in range(TOP_K):
                copy(dest_ref[(tile * tm + r) * TOP_K + kk], dst_slot, kk, r * ROW_TILE).start(priority=kk % 2)
            return carry

        lax.fori_loop(0, tm, start, 0, unroll=DMA_UNROLL)

    @pl.when(i == 0)
    def _():
        issue(0, 0)

    @pl.when(i + 1 < pl.num_programs(0))
    def _():
        issue(i + 1, 1 - slot)

    def wait(r, carry):
        for kk in range(TOP_K):
            copy(0, slot, kk, 0).wait()
        return carry

    lax.fori_loop(0, tm, wait, 0, unroll=WAIT_UNROLL)

    gt = gt_ref[...]
    rg = COMB_ROWS
    for g0 in range(0, tm, rg):
        rows = slice(g0, g0 + rg)
        parts = []
        for j in range(ROW_TILE):
            y = h_ref[rows, j * LANES:(j + 1) * LANES]
            for kk in range(TOP_K):
                y = y + gt[rows, kk:kk + 1] * buf_ref[slot, kk, pl.ds(g0 * ROW_TILE + j, rg, stride=ROW_TILE), :]
            parts.append(y)
        y = jnp.concatenate(parts, axis=1)
        o_ref[rows, :] = _rms(y, g_ref[...])


def _combine(dest_flat, h2, gates, final_g, yb):
    n, d = h2.shape
    tm = TM_COMB
    return pl.pallas_call(
        _combine_kernel,
        grid_spec=pltpu.PrefetchScalarGridSpec(
            num_scalar_prefetch=1,
            grid=(n // tm,),
            in_specs=[
                pl.BlockSpec((tm, d), lambda i, dest: (i, 0)),
                pl.BlockSpec((tm, TOP_K), lambda i, dest: (i, 0)),
                pl.BlockSpec((1, d), lambda i, dest: (0, 0)),
                pl.BlockSpec(memory_space=pl.ANY),
            ],
            out_specs=pl.BlockSpec((tm, d), lambda i, dest: (i, 0)),
            scratch_shapes=[pltpu.VMEM((2, TOP_K, tm * ROW_TILE, LANES), F32), pltpu.SemaphoreType.DMA((2,))],
        ),
        out_shape=jax.ShapeDtypeStruct((n, d), F32),
        compiler_params=_cparams(("arbitrary",)),
        name="combine",
    )(dest_flat, h2, gates, final_g.reshape(1, d), yb)


def kernel(x, mem, norm1_g, w_in, mu_shift, w0, w2, a0, a2, g2, k_k, k_a, r_k, lnx_g, lnx_b,
           lam_q1, lam_k1, lam_q2, lam_k2, subln_g, w_out, norm2_g, mem_norm_g, wq_x, wkv_x, wo_x,
           norm3_g, router_w, router_b, moe_w1, moe_b1, moe_w2, moe_b2, final_g):
    batch, seq, d = x.shape
    n = batch * seq
    shift_cols = mu_shift.shape[1]
    x2 = x.reshape(n, d)

    p, qk, vt = _inproj(x2, norm1_g[0], w_in[0], shift_cols)
    y_rwkv = _rwkv(p, batch, seq, mu_shift[0], w0[0], w2[0], a0[0], a2[0], g2[0], k_k[0], k_a[0],
                   r_k[0].reshape(-1), lnx_g[0], lnx_b[0])
    y_diff = _diffattn(qk, vt, batch, seq, lam_q1[0], lam_k1[0], lam_q2[0], lam_k2[0], subln_g[0])
    kv = _memkv(mem.reshape(-1, d), mem_norm_g[0], wkv_x[0])

    h2, xn3, top_i, gates, rank, counts = _mid(
        x2, y_rwkv, y_diff, batch, seq, w_out[0], norm2_g[0], wq_x[0], kv, wo_x[0], norm3_g[0],
        router_w[0], router_b[0])

    counts = counts.reshape(-1)
    padded = ((counts + MOE_TILE - 1) // MOE_TILE) * MOE_TILE
    padded_end = jnp.cumsum(padded)
    padded_off = padded_end - padded
    rows_padded = n * TOP_K + N_EXPERTS * MOE_TILE
    nblk = rows_padded // MOE_TILE
    block_start = jnp.arange(nblk, dtype=jnp.int32) * MOE_TILE
    block_e = jnp.minimum(jnp.sum(block_start[:, None] >= padded_end[None, :], axis=1),
                          N_EXPERTS - 1).astype(jnp.int32)
    nused = (padded_end[-1:] // MOE_TILE).astype(jnp.int32)
    expert_hot = top_i[:, :, None] == jnp.arange(N_EXPERTS, dtype=jnp.int32)
    row_off = jnp.sum(jnp.where(expert_hot, padded_off.astype(jnp.int32), 0), axis=-1)
    dest8 = ((row_off + rank) * ROW_TILE).astype(jnp.int32).reshape(-1)
    pad_start = jnp.concatenate([padded_off + counts, padded_end[-1:]]).astype(jnp.int32)
    pad_len = jnp.concatenate([padded - counts, rows_padded - padded_end[-1:]]).astype(jnp.int32)

    xin = _dispatch(dest8, pad_start, pad_len, xn3, rows_padded)
    yb = _experts(block_e, nused, xin, moe_w1[0], moe_b1[0], moe_w2[0], moe_b2[0])
    out = _combine(dest8, h2, gates, final_g, yb)
    return out.reshape(batch, seq, d)
```
